```python
import jax, jax.numpy as jnp
from jax import lax
import numpy as np

D_MODEL = 1024
BATCH = 8
SEQ = 4096
DEPTH = 2

CHUNK = 64
N_MIXERS = 2
N_RET = (DEPTH + 1) // 2
N_MLA = DEPTH // 2
ROPE_THETA = 10000.0
EPS = 1e-6
RET_HEADS = 4
RET_DK = D_MODEL // RET_HEADS
RET_DV = 2 * RET_DK
RET_QK_W = RET_HEADS * RET_DK
RET_V_W = RET_HEADS * RET_DV
RET_IN = 2 * RET_QK_W + 2 * RET_V_W
MLA_HEADS = 8
MLA_NOPE = 128
MLA_ROPE = 64
MLA_QKD = MLA_NOPE + MLA_ROPE
MLA_VD = 128
MLA_Q_RANK = 384
MLA_KV_RANK = 256
MLA_IN = MLA_Q_RANK + MLA_KV_RANK + MLA_ROPE
Q_BLOCK = 128
D_FF = 4 * D_MODEL
PLE_DIM = 256

kernel_name = "hybrid_retention_mla_trunk"


def _rmsnorm(x, g):
    xf = x.astype(jnp.float32)
    y = xf * lax.rsqrt(jnp.mean(xf * xf, axis=-1, keepdims=True) + EPS)
    return (y * g.astype(jnp.float32)).astype(x.dtype)


def _rope_tables(seq, dim, dtype):
    inv = 1.0 / (ROPE_THETA ** (jnp.arange(0, dim, 2, dtype=jnp.float32) / dim))
    ang = jnp.arange(seq, dtype=jnp.float32)[:, None] * inv[None, :]
    return jnp.cos(ang)[:, None, :].astype(dtype), jnp.sin(ang)[:, None, :].astype(dtype)


def _rope(x, cos, sin):
    x1, x2 = jnp.split(x, 2, axis=-1)
    return jnp.concatenate([x1 * cos - x2 * sin, x2 * cos + x1 * sin], axis=-1)


def _retention(h, w_in, gn_gain, w_out):
    B, S, _ = h.shape
    nc = S // CHUNK
    proj = h @ w_in
    q, k, v, g = jnp.split(proj, [RET_QK_W, 2 * RET_QK_W, 2 * RET_QK_W + RET_V_W], axis=-1)
    cos, sin = _rope_tables(S, RET_DK, h.dtype)
    q = _rope(q.reshape(B, S, RET_HEADS, RET_DK), cos, sin)
    k = _rope(k.reshape(B, S, RET_HEADS, RET_DK), cos, sin) * (RET_DK ** -0.5)
    v = v.reshape(B, S, RET_HEADS, RET_DV)

    def to_chunks(t):
        return t.reshape(B, nc, CHUNK, RET_HEADS, t.shape[-1]).transpose(1, 0, 3, 2, 4)

    qc, kc, vc = to_chunks(q), to_chunks(k), to_chunks(v)
    log_gamma = jnp.log(1.0 - 2.0 ** (-5.0 - jnp.arange(RET_HEADS, dtype=jnp.float32)))
    idx = jnp.arange(CHUNK, dtype=jnp.float32)
    dist = jnp.abs(idx[:, None] - idx[None, :])
    intra = jnp.exp(log_gamma[:, None, None] * dist).astype(h.dtype)
    q_decay = jnp.exp(log_gamma[:, None] * (idx + 1.0))[None, :, :, None].astype(h.dtype)
    k_decay = jnp.exp(log_gamma[:, None] * (CHUNK - 1.0 - idx))[None, :, :, None].astype(h.dtype)
    chunk_decay = jnp.exp(log_gamma * CHUNK)[None, :, None, None].astype(h.dtype)

    def step(state, xs):
        qi, ki, vi = xs
        scores = jnp.einsum('bhnd,bhmd->bhnm', qi, ki) * intra
        inner = jnp.einsum('bhnm,bhmv->bhnv', scores, vi)
        cross = jnp.einsum('bhnd,bhdv->bhnv', qi * q_decay, state)
        new_state = state * chunk_decay + jnp.einsum('bhmd,bhmv->bhdv', ki * k_decay, vi)
        return new_state, inner + cross

    state0 = jnp.zeros((B, RET_HEADS, RET_DK, RET_DV), h.dtype)
    _, out = lax.scan(step, state0, (qc, kc, vc))
    out = out.transpose(1, 0, 3, 2, 4).reshape(B, S, RET_HEADS, RET_DV)
    out = _rmsnorm(out, gn_gain).reshape(B, S, RET_V_W)
    return (jax.nn.silu(g) * out) @ w_out


def _mla(h, w_in, q_a_gain, kv_a_gain, w_uq, w_ukv, q_gain, k_gain, w_out):
    B, S, _ = h.shape
    proj = h @ w_in
    c_q, c_kv, k_r = jnp.split(proj, [MLA_Q_RANK, MLA_Q_RANK + MLA_KV_RANK], axis=-1)
    q = (_rmsnorm(c_q, q_a_gain) @ w_uq).reshape(B, S, MLA_HEADS, MLA_QKD)
    kv = (_rmsnorm(c_kv, kv_a_gain) @ w_ukv).reshape(B, S, MLA_HEADS, MLA_NOPE + MLA_VD)
    k_nope, v = jnp.split(kv, [MLA_NOPE], axis=-1)
    k = jnp.concatenate([k_nope, jnp.broadcast_to(k_r[:, :, None, :], (B, S, MLA_HEADS, MLA_ROPE))], axis=-1)
    q = _rmsnorm(q, q_gain)
    k = _rmsnorm(k, k_gain)
    cos, sin = _rope_tables(S, MLA_ROPE, h.dtype)
    q = jnp.concatenate([q[..., :MLA_NOPE], _rope(q[..., MLA_NOPE:], cos, sin)], axis=-1)
    k = jnp.concatenate([k[..., :MLA_NOPE], _rope(k[..., MLA_NOPE:], cos, sin)], axis=-1)
    q = q.transpose(0, 2, 1, 3)
    k = k.transpose(0, 2, 1, 3)
    v = v.transpose(0, 2, 1, 3)
    scale = MLA_QKD ** -0.5
    outs = []
    for blk in range(S // Q_BLOCK):
        q0 = blk * Q_BLOCK
        kend = q0 + Q_BLOCK
        qb = q[:, :, q0:kend]
        kb = k[:, :, :kend]
        vb = v[:, :, :kend]
        s = jnp.einsum('bhqd,bhkd->bhqk', qb, kb).astype(jnp.float32) * scale
        q_chunk = (q0 + jnp.arange(Q_BLOCK)) // CHUNK
        k_chunk = jnp.arange(kend) // CHUNK
        s = jnp.where(k_chunk[None, :] <= q_chunk[:, None], s, -1e30)
        pr = jax.nn.softmax(s, axis=-1).astype(vb.dtype)
        outs.append(jnp.einsum('bhqk,bhkd->bhqd', pr, vb))
    o = jnp.concatenate(outs, axis=2).transpose(0, 2, 1, 3).reshape(B, S, MLA_HEADS * MLA_VD)
    return o @ w_out


def _dense(k, shape, fan_in):
    return jax.random.normal(k, shape, jnp.float32) * (fan_in ** -0.5)


def _gain(k, shape):
    return 1.0 + 0.05 * jax.random.normal(k, shape, jnp.float32)


def setup_inputs(seed: int = 0) -> dict:
    key = jax.random.key(seed)
    ks = jax.random.split(key, 20)
    return {
        "x": jax.random.normal(ks[0], (BATCH, SEQ, D_MODEL), jnp.float32),
        "p": jax.random.normal(ks[1], (DEPTH, BATCH, SEQ, PLE_DIM), jnp.float32),
        "mix_norm": _gain(ks[2], (DEPTH, D_MODEL)),
        "ret_w_in": _dense(ks[3], (N_RET, D_MODEL, RET_IN), D_MODEL),
        "ret_gn": _gain(ks[4], (N_RET, RET_HEADS, RET_DV)),
        "ret_w_out": _dense(ks[5], (N_RET, RET_V_W, D_MODEL), RET_V_W),
        "mla_w_in": _dense(ks[6], (N_MLA, D_MODEL, MLA_IN), D_MODEL),
        "mla_q_a_norm": _gain(ks[7], (N_MLA, MLA_Q_RANK)),
        "mla_kv_a_norm": _gain(ks[8], (N_MLA, MLA_KV_RANK)),
        "mla_w_uq": _dense(ks[9], (N_MLA, MLA_Q_RANK, MLA_HEADS * MLA_QKD), MLA_Q_RANK),
        "mla_w_ukv": _dense(ks[10], (N_MLA, MLA_KV_RANK, MLA_HEADS * (MLA_NOPE + MLA_VD)), MLA_KV_RANK),
        "mla_q_norm": _gain(ks[11], (N_MLA, MLA_QKD)),
        "mla_k_norm": _gain(ks[12], (N_MLA, MLA_QKD)),
        "mla_w_out": _dense(ks[13], (N_MLA, MLA_HEADS * MLA_VD, D_MODEL), MLA_HEADS * MLA_VD),
        "mlp_norm": _gain(ks[14], (DEPTH, D_MODEL)),
        "mlp_w1": _dense(ks[15], (DEPTH, D_MODEL, D_FF), D_MODEL),
        "mlp_w2": _dense(ks[16], (DEPTH, D_FF, D_MODEL), D_FF),
        "ple_norm": _gain(ks[17], (DEPTH, D_MODEL)),
        "ple_gate_w": _dense(ks[18], (DEPTH, D_MODEL, D_MODEL), D_MODEL),
        "ple_proj_w": _dense(ks[19], (DEPTH, PLE_DIM, D_MODEL), PLE_DIM),
    }


def reference(x, p, mix_norm, ret_w_in, ret_gn, ret_w_out, mla_w_in, mla_q_a_norm, mla_kv_a_norm,
              mla_w_uq, mla_w_ukv, mla_q_norm, mla_k_norm, mla_w_out, mlp_norm, mlp_w1, mlp_w2,
              ple_norm, ple_gate_w, ple_proj_w):
    h = x
    for i in range(DEPTH):
        j = i // N_MIXERS
        hn = _rmsnorm(h, mix_norm[i])
        if i % N_MIXERS == 0:
            mixed = _retention(hn, ret_w_in[j], ret_gn[j], ret_w_out[j])
        else:
            mixed = _mla(hn, mla_w_in[j], mla_q_a_norm[j], mla_kv_a_norm[j], mla_w_uq[j], mla_w_ukv[j],
                         mla_q_norm[j], mla_k_norm[j], mla_w_out[j])
        h = h + mixed
        hn = _rmsnorm(h, mlp_norm[i])
        h = h + jnp.square(jax.nn.relu(hn @ mlp_w1[i])) @ mlp_w2[i]
        gate = jax.nn.sigmoid(_rmsnorm(h, ple_norm[i]) @ ple_gate_w[i])
        h = h + gate * (p[i] @ ple_proj_w[i])
    return h
```

```python
import functools

import jax
import jax.numpy as jnp
from jax import lax
from jax.experimental import pallas as pl
from jax.experimental.pallas import tpu as pltpu

EPS = 1e-6
ROPE_THETA = 10000.0
CHUNK = 64
LANES = 128

RET_HEADS = 4
RET_DK = 256
RET_DV = 512
RET_SUPER = 256

MLA_HEADS = 8
MLA_NOPE = 128
MLA_ROPE = 64
MLA_QKD = MLA_NOPE + MLA_ROPE
MLA_VD = 128
MLA_Q_RANK = 384
MLA_KV_RANK = 256
MLA_HEAD_W = 256

MXU_DTYPE = jnp.bfloat16
F32 = jnp.float32
VMEM_LIMIT = 56 * 1024 * 1024

_NT = (((1,), (1,)), ((), ()))
_TN = (((0,), (0,)), ((), ()))


def _dot(a, b):
    return jnp.dot(a, b, preferred_element_type=F32)


def _rms_scale(x):
    return lax.rsqrt(jnp.mean(x * x, axis=-1, keepdims=True) + EPS)


def _sigmoid(x):
    return 1.0 / (1.0 + jnp.exp(-x))


def _resident(shape):
    zeros = (0,) * len(shape)
    return pl.BlockSpec(shape, lambda *_: zeros, pipeline_mode=pl.Buffered(1))


def _params(*semantics):
    return pltpu.CompilerParams(dimension_semantics=semantics, vmem_limit_bytes=VMEM_LIMIT)


def _rope_angles(seq, dim):
    inv = 1.0 / (ROPE_THETA ** (jnp.arange(0, dim, 2, dtype=F32) / dim))
    ang = jnp.arange(seq, dtype=F32)[:, None] * inv[None, :]
    return jnp.cos(ang), jnp.sin(ang)


def _ret_proj_kernel(x_ref, g_ref, w_ref, cos_ref, sin_ref, q_ref, k_ref, v_ref, gate_ref):
    x = x_ref[...]
    hn = (x * _rms_scale(x) * g_ref[...]).astype(MXU_DTYPE)
    cos = cos_ref[...]
    sin = sin_ref[...]
    half = RET_DK // 2
    qk_w = RET_HEADS * RET_DK
    k_scale = RET_DK ** -0.5
    for h in range(RET_HEADS):
        lo = h * RET_DK
        r = _dot(hn, w_ref[:, lo:lo + RET_DK])
        x1, x2 = r[:, :half], r[:, half:]
        q_ref[:, lo:lo + half] = (x1 * cos - x2 * sin).astype(q_ref.dtype)
        q_ref[:, lo + half:lo + RET_DK] = (x2 * cos + x1 * sin).astype(q_ref.dtype)
        r = _dot(hn, w_ref[:, qk_w + lo:qk_w + lo + RET_DK])
        x1, x2 = r[:, :half], r[:, half:]
        k_ref[:, lo:lo + half] = ((x1 * cos - x2 * sin) * k_scale).astype(k_ref.dtype)
        k_ref[:, lo + half:lo + RET_DK] = ((x2 * cos + x1 * sin) * k_scale).astype(k_ref.dtype)
    v_w = RET_HEADS * RET_DV
    for h in range(RET_HEADS):
        lo = h * RET_DV
        v_ref[:, lo:lo + RET_DV] = _dot(hn, w_ref[:, 2 * qk_w + lo:2 * qk_w + lo + RET_DV]).astype(v_ref.dtype)
        gate_ref[:, lo:lo + RET_DV] = _dot(
            hn, w_ref[:, 2 * qk_w + v_w + lo:2 * qk_w + v_w + lo + RET_DV]).astype(gate_ref.dtype)


def _ret_proj(x2d, gain, w_in, seq, tm):
    t, d = x2d.shape
    qk_w = RET_HEADS * RET_DK
    v_w = RET_HEADS * RET_DV
    cos, sin = _rope_angles(seq, RET_DK)
    pos_blocks = seq // tm
    row = lambda i: (i, 0)
    pos = lambda i: (i % pos_blocks, 0)
    return pl.pallas_call(
        _ret_proj_kernel,
        grid=(t // tm,),
        in_specs=[
            pl.BlockSpec((tm, d), row),
            _resident((1, d)),
            _resident(w_in.shape),
            pl.BlockSpec((tm, RET_DK // 2), pos),
            pl.BlockSpec((tm, RET_DK // 2), pos),
        ],
        out_specs=[
            pl.BlockSpec((tm, qk_w), row),
            pl.BlockSpec((tm, qk_w), row),
            pl.BlockSpec((tm, v_w), row),
            pl.BlockSpec((tm, v_w), row),
        ],
        out_shape=[
            jax.ShapeDtypeStruct((t, qk_w), MXU_DTYPE),
            jax.ShapeDtypeStruct((t, qk_w), MXU_DTYPE),
            jax.ShapeDtypeStruct((t, v_w), MXU_DTYPE),
            jax.ShapeDtypeStruct((t, v_w), MXU_DTYPE),
        ],
        compiler_params=_params("parallel"),
        name="ret_proj",
    )(x2d, gain.reshape(1, d), w_in.astype(MXU_DTYPE), cos, sin)


def _retention_kernel(q_ref, k_ref, v_ref, gate_ref, decay_ref, qd_ref, kd_ref, cd_ref, gn_ref,
                      y_ref, state_ref, *, groups):
    @pl.when(pl.program_id(1) == 0)
    def _():
        state_ref[...] = jnp.zeros_like(state_ref)

    for c in range(groups):
        rows = slice(c * RET_SUPER, (c + 1) * RET_SUPER)
        for h in range(RET_HEADS):
            qk_cols = slice(h * RET_DK, (h + 1) * RET_DK)
            v_cols = slice(h * RET_DV, (h + 1) * RET_DV)
            q = q_ref[0, rows, qk_cols]
            k = k_ref[0, rows, qk_cols]
            v = v_ref[0, rows, v_cols]
            state = state_ref[h]
            scores = lax.dot_general(q, k, _NT, preferred_element_type=F32) * decay_ref[h]
            inner = _dot(scores.astype(MXU_DTYPE), v)
            cross = _dot(q, state.astype(MXU_DTYPE)) * qd_ref[h]
            out = inner + cross
            k_dec = (k.astype(F32) * kd_ref[h]).astype(MXU_DTYPE)
            state_ref[h] = state * cd_ref[h] + lax.dot_general(k_dec, v, _TN, preferred_element_type=F32)
            normed = out * _rms_scale(out) * gn_ref[h]
            gate = gate_ref[0, rows, v_cols].astype(F32)
            y_ref[0, rows, v_cols] = (gate * _sigmoid(gate) * normed).astype(y_ref.dtype)


def _retention_tables():
    log_gamma = jnp.log(1.0 - 2.0 ** (-5.0 - jnp.arange(RET_HEADS, dtype=F32)))
    idx = jnp.arange(RET_SUPER, dtype=F32)
    dist = jnp.abs(idx[:, None] - idx[None, :])
    chunk = jnp.arange(RET_SUPER) // CHUNK
    visible = chunk[None, :] <= chunk[:, None]
    decay = jnp.where(visible[None], jnp.exp(log_gamma[:, None, None] * dist), 0.0)
    q_decay = jnp.exp(log_gamma[:, None] * (idx + 1.0))
    k_decay = jnp.exp(log_gamma[:, None] * (RET_SUPER - 1.0 - idx))
    group_decay = jnp.exp(log_gamma * RET_SUPER)
    qd = jnp.broadcast_to(q_decay[:, :, None], (RET_HEADS, RET_SUPER, RET_DV))
    kd = jnp.broadcast_to(k_decay[:, :, None], (RET_HEADS, RET_SUPER, RET_DK))
    cd = jnp.broadcast_to(group_decay[:, None, None], (RET_HEADS, 1, RET_DV))
    return decay.astype(F32), qd.astype(F32), kd.astype(F32), cd.astype(F32)


def _retention(q, k, v, gate, gn_gain, block):
    b, s, _ = q.shape
    qk_w = RET_HEADS * RET_DK
    v_w = RET_HEADS * RET_DV
    decay, qd, kd, cd = _retention_tables()
    tok = lambda i, j: (i, j, 0)
    return pl.pallas_call(
        functools.partial(_retention_kernel, groups=block // RET_SUPER),
        grid=(b, s // block),
        in_specs=[
            pl.BlockSpec((1, block, qk_w), tok),
            pl.BlockSpec((1, block, qk_w), tok),
            pl.BlockSpec((1, block, v_w), tok),
            pl.BlockSpec((1, block, v_w), tok),
            _resident(decay.shape),
            _resident(qd.shape),
            _resident(kd.shape),
            _resident(cd.shape),
            _resident((RET_HEADS, 1, RET_DV)),
        ],
        out_specs=pl.BlockSpec((1, block, v_w), tok),
        out_shape=jax.ShapeDtypeStruct((b, s, v_w), MXU_DTYPE),
        scratch_shapes=[pltpu.VMEM((RET_HEADS, RET_DK, RET_DV), F32)],
        compiler_params=_params("parallel", "arbitrary"),
        name="retention",
    )(q, k, v, gate, decay, qd, kd, cd, gn_gain.reshape(RET_HEADS, 1, RET_DV))


def _tail_kernel(h_ref, y_ref, p_ref, wo_ref, mg_ref, w1_ref, w2_ref, pg_ref, wg_ref, wp_ref, o_ref,
                 *, ff_chunk):
    h = h_ref[...] + _dot(y_ref[...], wo_ref[...])
    hn = (h * _rms_scale(h) * mg_ref[...]).astype(MXU_DTYPE)
    d_ff = w1_ref.shape[1]
    for c in range(d_ff // ff_chunk):
        cols = slice(c * ff_chunk, (c + 1) * ff_chunk)
        a = jnp.square(jnp.maximum(_dot(hn, w1_ref[:, cols]), 0.0))
        h = h + _dot(a.astype(MXU_DTYPE), w2_ref[cols, :])
    hn = (h * _rms_scale(h) * pg_ref[...]).astype(MXU_DTYPE)
    gate = _sigmoid(_dot(hn, wg_ref[...]))
    emb = _dot(p_ref[...].astype(MXU_DTYPE), wp_ref[...])
    o_ref[...] = h + gate * emb


def _tail(h2d, y2d, p2d, w_out, mlp_gain, w1, w2, ple_gain, wg, wp, tm, ff_chunk):
    t, d = h2d.shape
    row = lambda i: (i, 0)
    return pl.pallas_call(
        functools.partial(_tail_kernel, ff_chunk=ff_chunk),
        grid=(t // tm,),
        in_specs=[
            pl.BlockSpec((tm, d), row),
            pl.BlockSpec((tm, y2d.shape[1]), row),
            pl.BlockSpec((tm, p2d.shape[1]), row),
            _resident(w_out.shape),
            _resident((1, d)),
            _resident(w1.shape),
            _resident(w2.shape),
            _resident((1, d)),
            _resident(wg.shape),
            _resident(wp.shape),
        ],
        out_specs=pl.BlockSpec((tm, d), row),
        out_shape=jax.ShapeDtypeStruct((t, d), F32),
        compiler_params=_params("parallel"),
        name="layer_tail",
    )(h2d, y2d, p2d, w_out.astype(MXU_DTYPE), mlp_gain.reshape(1, d), w1.astype(MXU_DTYPE),
      w2.astype(MXU_DTYPE), ple_gain.reshape(1, d), wg.astype(MXU_DTYPE), wp.astype(MXU_DTYPE))


def _mla_proj_kernel(x_ref, g_ref, win_ref, qa_ref, kva_ref, wuq_ref, wukv_ref, qn_ref, kn_ref,
                     qrope_ref, krope_ref, q_ref, k_ref, v_ref):
    x = x_ref[...]
    hn = (x * _rms_scale(x) * g_ref[...]).astype(MXU_DTYPE)
    proj = _dot(hn, win_ref[...])
    c_q = proj[:, :MLA_Q_RANK]
    c_kv = proj[:, MLA_Q_RANK:MLA_Q_RANK + MLA_KV_RANK]
    k_r = proj[:, MLA_Q_RANK + MLA_KV_RANK:]
    c_q = (c_q * _rms_scale(c_q) * qa_ref[...]).astype(MXU_DTYPE)
    c_kv = (c_kv * _rms_scale(c_kv) * kva_ref[...]).astype(MXU_DTYPE)

    lane = lax.broadcasted_iota(jnp.int32, (1, LANES), 1)
    first_half = (lane < MLA_ROPE).astype(F32)
    softmax_scale = MLA_QKD ** -0.5
    qn = qn_ref[...]
    kn = kn_ref[...]

    k_rot = k_r * krope_ref[...]
    k_rot = (k_rot + pltpu.roll(k_rot, MLA_ROPE, 1)) * first_half
    k_r_ss = 0.5 * jnp.sum(k_r * k_r, axis=-1, keepdims=True)

    for h in range(MLA_HEADS):
        lo = h * MLA_HEAD_W
        qh = _dot(c_q, wuq_ref[:, lo:lo + MLA_HEAD_W])
        q_nope, q_r = qh[:, :MLA_NOPE], qh[:, MLA_NOPE:]
        ss = jnp.sum(q_nope * q_nope, axis=-1, keepdims=True) + 0.5 * jnp.sum(q_r * q_r, axis=-1, keepdims=True)
        inv = lax.rsqrt(ss * (1.0 / MLA_QKD) + EPS) * softmax_scale
        q_rot = q_r * qrope_ref[...]
        q_rot = q_rot + pltpu.roll(q_rot, MLA_ROPE, 1)
        q_ref[:, lo:lo + MLA_NOPE] = (q_nope * inv * qn).astype(q_ref.dtype)
        q_ref[:, lo + MLA_NOPE:lo + MLA_HEAD_W] = (q_rot * inv).astype(q_ref.dtype)

        kvh = _dot(c_kv, wukv_ref[:, lo:lo + MLA_HEAD_W])
        k_nope, v = kvh[:, :MLA_NOPE], kvh[:, MLA_NOPE:]
        ss = jnp.sum(k_nope * k_nope, axis=-1, keepdims=True) + k_r_ss
        inv = lax.rsqrt(ss * (1.0 / MLA_QKD) + EPS)
        k_ref[:, lo:lo + MLA_NOPE] = (k_nope * inv * kn).astype(k_ref.dtype)
        k_ref[:, lo + MLA_NOPE:lo + MLA_HEAD_W] = (k_rot * inv).astype(k_ref.dtype)
        v_ref[:, h * MLA_VD:(h + 1) * MLA_VD] = v.astype(v_ref.dtype)


def _rotate_half_cols(w):
    half = w.shape[-1] // 2
    return jnp.concatenate([-w[..., half:], w[..., :half]], axis=-1)


def _rope_gain_table(seq, gain_rope):
    cos, sin = _rope_angles(seq, MLA_ROPE)
    cos = jnp.concatenate([cos, cos], axis=-1)
    sin = jnp.concatenate([sin, sin], axis=-1)
    half = MLA_ROPE // 2
    gain_rot = jnp.concatenate([gain_rope[half:], gain_rope[:half]])
    return jnp.concatenate([cos * gain_rope[None, :], sin * gain_rot[None, :]], axis=-1)


def _mla_proj(x2d, gain, w_in, q_a_gain, kv_a_gain, w_uq, w_ukv, q_gain, k_gain, seq, tm):
    t, d = x2d.shape
    kr_lo = MLA_Q_RANK + MLA_KV_RANK
    w_in_ext = jnp.concatenate([w_in, _rotate_half_cols(w_in[:, kr_lo:])], axis=1)
    w_uq_h = w_uq.reshape(MLA_Q_RANK, MLA_HEADS, MLA_QKD)
    w_uq_ext = jnp.concatenate([w_uq_h, _rotate_half_cols(w_uq_h[..., MLA_NOPE:])], axis=-1)
    w_uq_ext = w_uq_ext.reshape(MLA_Q_RANK, MLA_HEADS * MLA_HEAD_W)
    q_rope = _rope_gain_table(seq, q_gain[MLA_NOPE:])
    k_rope = _rope_gain_table(seq, k_gain[MLA_NOPE:])
    qk_w = MLA_HEADS * MLA_HEAD_W
    v_w = MLA_HEADS * MLA_VD
    pos_blocks = seq // tm
    row = lambda i: (i, 0)
    pos = lambda i: (i % pos_blocks, 0)
    return pl.pallas_call(
        _mla_proj_kernel,
        grid=(t // tm,),
        in_specs=[
            pl.BlockSpec((tm, d), row),
            _resident((1, d)),
            _resident(w_in_ext.shape),
            _resident((1, MLA_Q_RANK)),
            _resident((1, MLA_KV_RANK)),
            _resident(w_uq_ext.shape),
            _resident(w_ukv.shape),
            _resident((1, MLA_NOPE)),
            _resident((1, MLA_NOPE)),
            pl.BlockSpec((tm, LANES), pos),
            pl.BlockSpec((tm, LANES), pos),
        ],
        out_specs=[
            pl.BlockSpec((tm, qk_w), row),
            pl.BlockSpec((tm, qk_w), row),
            pl.BlockSpec((tm, v_w), row),
        ],
        out_shape=[
            jax.ShapeDtypeStruct((t, qk_w), MXU_DTYPE),
            jax.ShapeDtypeStruct((t, qk_w), MXU_DTYPE),
            jax.ShapeDtypeStruct((t, v_w), MXU_DTYPE),
        ],
        compiler_params=_params("parallel"),
        name="mla_proj",
    )(x2d, gain.reshape(1, d), w_in_ext.astype(MXU_DTYPE), q_a_gain.reshape(1, -1), kv_a_gain.reshape(1, -1),
      w_uq_ext.astype(MXU_DTYPE), w_ukv.astype(MXU_DTYPE), q_gain[:MLA_NOPE].reshape(1, -1),
      k_gain[:MLA_NOPE].reshape(1, -1), q_rope, k_rope)


def _attn_kernel(q_ref, k_ref, v_ref, o_ref, m_ref, l_ref, acc_ref, *, block):
    qi = pl.program_id(2)
    q = q_ref[0]
    m_ref[...] = jnp.full_like(m_ref, -1e30)
    l_ref[...] = jnp.zeros_like(l_ref)
    acc_ref[...] = jnp.zeros_like(acc_ref)

    def update(start, masked):
        k = k_ref[0, pl.ds(start, block), :]
        v = v_ref[0, pl.ds(start, block), :]
        s = lax.dot_general(q, k, _NT, preferred_element_type=F32)
        if masked:
            q_chunk = lax.broadcasted_iota(jnp.int32, s.shape, 0) // CHUNK
            k_chunk = lax.broadcasted_iota(jnp.int32, s.shape, 1) // CHUNK
            s = jnp.where(k_chunk <= q_chunk, s, -1e30)
        m_prev = m_ref[...]
        m_new = jnp.maximum(m_prev, jnp.max(s, axis=-1, keepdims=True))
        alpha = jnp.exp(m_prev - m_new)
        p = jnp.exp(s - m_new)
        l_ref[...] = alpha * l_ref[...] + jnp.sum(p, axis=-1, keepdims=True)
        acc_ref[...] = alpha * acc_ref[...] + _dot(p.astype(MXU_DTYPE), v)
        m_ref[...] = m_new

    def body(j, carry):
        update(pl.multiple_of(j * block, block), False)
        return carry

    lax.fori_loop(0, qi, body, 0)
    update(pl.multiple_of(qi * block, block), True)
    o_ref[0] = (acc_ref[...] / l_ref[...]).astype(o_ref.dtype)


def _attention(q, k, v, block):
    b, s, _ = q.shape
    return pl.pallas_call(
        functools.partial(_attn_kernel, block=block),
        grid=(b, MLA_HEADS, s // block),
        in_specs=[
            pl.BlockSpec((1, block, MLA_HEAD_W), lambda i, h, j: (i, j, h)),
            pl.BlockSpec((1, s, MLA_HEAD_W), lambda i, h, j: (i, 0, h)),
            pl.BlockSpec((1, s, MLA_VD), lambda i, h, j: (i, 0, h)),
        ],
        out_specs=pl.BlockSpec((1, block, MLA_VD), lambda i, h, j: (i, j, h)),
        out_shape=jax.ShapeDtypeStruct((b, s, MLA_HEADS * MLA_VD), MXU_DTYPE),
        scratch_shapes=[
            pltpu.VMEM((block, 1), F32),
            pltpu.VMEM((block, 1), F32),
            pltpu.VMEM((block, MLA_VD), F32),
        ],
        compiler_params=_params("parallel", "parallel", "arbitrary"),
        name="mla_attention",
    )(q, k, v)


def _tiles(seq):
    assert seq % RET_SUPER == 0, seq
    tm = 512 if seq % 512 == 0 else RET_SUPER
    return dict(tm=tm, ret_block=tm, attn_block=tm, ff_chunk=1024)


def kernel(x, p, mix_norm, ret_w_in, ret_gn, ret_w_out, mla_w_in, mla_q_a_norm, mla_kv_a_norm, mla_w_uq,
           mla_w_ukv, mla_q_norm, mla_k_norm, mla_w_out, mlp_norm, mlp_w1, mlp_w2, ple_norm, ple_gate_w,
           ple_proj_w):
    b, s, d = x.shape
    depth = p.shape[0]
    t = b * s
    cfg = _tiles(s)
    tm = cfg["tm"]
    h = x.reshape(t, d)
    for i in range(depth):
        j = i // 2
        if i % 2 == 0:
            q, k, v, gate = _ret_proj(h, mix_norm[i], ret_w_in[j], s, tm)
            y = _retention(q.reshape(b, s, -1), k.reshape(b, s, -1), v.reshape(b, s, -1),
                           gate.reshape(b, s, -1), ret_gn[j], cfg["ret_block"])
            w_out = ret_w_out[j]
        else:
            q, k, v = _mla_proj(h, mix_norm[i], mla_w_in[j], mla_q_a_norm[j], mla_kv_a_norm[j], mla_w_uq[j],
                                mla_w_ukv[j], mla_q_norm[j], mla_k_norm[j], s, tm)
            y = _attention(q.reshape(b, s, -1), k.reshape(b, s, -1), v.reshape(b, s, -1), cfg["attn_block"])
            w_out = mla_w_out[j]
        h = _tail(h, y.reshape(t, -1), p[i].reshape(t, -1), w_out, mlp_norm[i], mlp_w1[i], mlp_w2[i],
                  ple_norm[i], ple_gate_w[i], ple_proj_w[i], tm, cfg["ff_chunk"])
    return h.reshape(b, s, d)
```

```python
import functools

import jax
import jax.numpy as jnp
from jax import lax
from jax.experimental import pallas as pl
from jax.experimental.pallas import tpu as pltpu

EPS = 1e-6
ROPE_THETA = 10000.0
CHUNK = 64
LANES = 128

RET_HEADS = 4
RET_DK = 256
RET_DV = 512
RET_SUPER = 256

MLA_HEADS = 8
MLA_NOPE = 128
MLA_ROPE = 64
MLA_QKD = MLA_NOPE + MLA_ROPE
MLA_VD = 128
MLA_Q_RANK = 384
MLA_KV_RANK = 256
MLA_HEAD_W = 256

MXU_DTYPE = jnp.bfloat16
F32 = jnp.float32
LOG2_E = 1.4426950408889634
VMEM_LIMIT = 56 * 1024 * 1024

_NT = (((1,), (1,)), ((), ()))
_TN = (((0,), (0,)), ((), ()))


def _dot(a, b):
    return jnp.dot(a, b, preferred_element_type=F32)


def _rms_scale(x):
    return lax.rsqrt(jnp.mean(x * x, axis=-1, keepdims=True) + EPS)


def _sigmoid(x):
    return 1.0 / (1.0 + jnp.exp(-x))


def _resident(shape):
    zeros = (0,) * len(shape)
    return pl.BlockSpec(shape, lambda *_: zeros, pipeline_mode=pl.Buffered(1))


def _params(*semantics):
    return pltpu.CompilerParams(dimension_semantics=semantics, vmem_limit_bytes=VMEM_LIMIT)


def _rope_angles(seq, dim):
    inv = 1.0 / (ROPE_THETA ** (jnp.arange(0, dim, 2, dtype=F32) / dim))
    ang = jnp.arange(seq, dtype=F32)[:, None] * inv[None, :]
    return jnp.cos(ang), jnp.sin(ang)


def _ret_proj_kernel(x_ref, g_ref, w_ref, cos_ref, sin_ref, q_ref, k_ref, v_ref, gate_ref):
    x = x_ref[...]
    hn = (x * _rms_scale(x) * g_ref[...]).astype(MXU_DTYPE)
    cos = cos_ref[...]
    sin = sin_ref[...]
    half = RET_DK // 2
    qk_w = RET_HEADS * RET_DK
    k_scale = RET_DK ** -0.5
    for h in range(RET_HEADS):
        lo = h * RET_DK
        r = _dot(hn, w_ref[:, lo:lo + RET_DK])
        x1, x2 = r[:, :half], r[:, half:]
        q_ref[:, lo:lo + half] = (x1 * cos - x2 * sin).astype(q_ref.dtype)
        q_ref[:, lo + half:lo + RET_DK] = (x2 * cos + x1 * sin).astype(q_ref.dtype)
        r = _dot(hn, w_ref[:, qk_w + lo:qk_w + lo + RET_DK])
        x1, x2 = r[:, :half], r[:, half:]
        k_ref[:, lo:lo + half] = ((x1 * cos - x2 * sin) * k_scale).astype(k_ref.dtype)
        k_ref[:, lo + half:lo + RET_DK] = ((x2 * cos + x1 * sin) * k_scale).astype(k_ref.dtype)
    v_w = RET_HEADS * RET_DV
    for h in range(RET_HEADS):
        lo = h * RET_DV
        v_ref[:, lo:lo + RET_DV] = _dot(hn, w_ref[:, 2 * qk_w + lo:2 * qk_w + lo + RET_DV]).astype(v_ref.dtype)
        gate_ref[:, lo:lo + RET_DV] = _dot(
            hn, w_ref[:, 2 * qk_w + v_w + lo:2 * qk_w + v_w + lo + RET_DV]).astype(gate_ref.dtype)


def _ret_proj(x2d, gain, w_in, seq, tm):
    t, d = x2d.shape
    qk_w = RET_HEADS * RET_DK
    v_w = RET_HEADS * RET_DV
    cos, sin = _rope_angles(seq, RET_DK)
    pos_blocks = seq // tm
    row = lambda i: (i, 0)
    pos = lambda i: (i % pos_blocks, 0)
    return pl.pallas_call(
        _ret_proj_kernel,
        grid=(t // tm,),
        in_specs=[
            pl.BlockSpec((tm, d), row),
            _resident((1, d)),
            _resident(w_in.shape),
            pl.BlockSpec((tm, RET_DK // 2), pos),
            pl.BlockSpec((tm, RET_DK // 2), pos),
        ],
        out_specs=[
            pl.BlockSpec((tm, qk_w), row),
            pl.BlockSpec((tm, qk_w), row),
            pl.BlockSpec((tm, v_w), row),
            pl.BlockSpec((tm, v_w), row),
        ],
        out_shape=[
            jax.ShapeDtypeStruct((t, qk_w), MXU_DTYPE),
            jax.ShapeDtypeStruct((t, qk_w), MXU_DTYPE),
            jax.ShapeDtypeStruct((t, v_w), MXU_DTYPE),
            jax.ShapeDtypeStruct((t, v_w), MXU_DTYPE),
        ],
        compiler_params=_params("parallel"),
        name="ret_proj",
    )(x2d, gain.reshape(1, d), w_in.astype(MXU_DTYPE), cos, sin)


def _retention_kernel(q_ref, k_ref, v_ref, gate_ref, decay_ref, qd_ref, kd_ref, cd_ref, gn_ref,
                      y_ref, state_ref, *, groups):
    @pl.when(pl.program_id(1) == 0)
    def _():
        state_ref[...] = jnp.zeros_like(state_ref)

    for c in range(groups):
        rows = slice(c * RET_SUPER, (c + 1) * RET_SUPER)
        for h in range(RET_HEADS):
            qk_cols = slice(h * RET_DK, (h + 1) * RET_DK)
            v_cols = slice(h * RET_DV, (h + 1) * RET_DV)
            q = q_ref[0, rows, qk_cols]
            k = k_ref[0, rows, qk_cols]
            v = v_ref[0, rows, v_cols]
            state = state_ref[h]
            scores = lax.dot_general(q, k, _NT, preferred_element_type=F32) * decay_ref[h]
            inner = _dot(scores.astype(MXU_DTYPE), v)
            cross = _dot(q, state.astype(MXU_DTYPE)) * qd_ref[h]
            out = inner + cross
            k_dec = (k.astype(F32) * kd_ref[h]).astype(MXU_DTYPE)
            state_ref[h] = state * cd_ref[h] + lax.dot_general(k_dec, v, _TN, preferred_element_type=F32)
            normed = out * _rms_scale(out) * gn_ref[h]
            gate = gate_ref[0, rows, v_cols].astype(F32)
            y_ref[0, rows, v_cols] = (gate * _sigmoid(gate) * normed).astype(y_ref.dtype)


def _retention_tables():
    log_gamma = jnp.log(1.0 - 2.0 ** (-5.0 - jnp.arange(RET_HEADS, dtype=F32)))
    idx = jnp.arange(RET_SUPER, dtype=F32)
    dist = jnp.abs(idx[:, None] - idx[None, :])
    chunk = jnp.arange(RET_SUPER) // CHUNK
    visible = chunk[None, :] <= chunk[:, None]
    decay = jnp.where(visible[None], jnp.exp(log_gamma[:, None, None] * dist), 0.0)
    q_decay = jnp.exp(log_gamma[:, None] * (idx + 1.0))
    k_decay = jnp.exp(log_gamma[:, None] * (RET_SUPER - 1.0 - idx))
    group_decay = jnp.exp(log_gamma * RET_SUPER)
    qd = jnp.broadcast_to(q_decay[:, :, None], (RET_HEADS, RET_SUPER, RET_DV))
    kd = jnp.broadcast_to(k_decay[:, :, None], (RET_HEADS, RET_SUPER, RET_DK))
    cd = jnp.broadcast_to(group_decay[:, None, None], (RET_HEADS, 1, RET_DV))
    return decay.astype(F32), qd.astype(F32), kd.astype(F32), cd.astype(F32)


def _retention(q, k, v, gate, gn_gain, block):
    b, s, _ = q.shape
    qk_w = RET_HEADS * RET_DK
    v_w = RET_HEADS * RET_DV
    decay, qd, kd, cd = _retention_tables()
    tok = lambda i, j: (i, j, 0)
    return pl.pallas_call(
        functools.partial(_retention_kernel, groups=block // RET_SUPER),
        grid=(b, s // block),
        in_specs=[
            pl.BlockSpec((1, block, qk_w), tok),
            pl.BlockSpec((1, block, qk_w), tok),
            pl.BlockSpec((1, block, v_w), tok),
            pl.BlockSpec((1, block, v_w), tok),
            _resident(decay.shape),
            _resident(qd.shape),
            _resident(kd.shape),
            _resident(cd.shape),
            _resident((RET_HEADS, 1, RET_DV)),
        ],
        out_specs=pl.BlockSpec((1, block, v_w), tok),
        out_shape=jax.ShapeDtypeStruct((b, s, v_w), MXU_DTYPE),
        scratch_shapes=[pltpu.VMEM((RET_HEADS, RET_DK, RET_DV), F32)],
        compiler_params=_params("parallel", "arbitrary"),
        name="retention",
    )(q, k, v, gate, decay, qd, kd, cd, gn_gain.reshape(RET_HEADS, 1, RET_DV))


def _tail_kernel(h_ref, y_ref, p_ref, wo_ref, mg_ref, w1_ref, w2_ref, pg_ref, wg_ref, wp_ref, o_ref,
                 *, ff_chunk):
    h = h_ref[...] + _dot(y_ref[...], wo_ref[...])
    hn = (h * _rms_scale(h) * mg_ref[...]).astype(MXU_DTYPE)
    d_ff = w1_ref.shape[1]
    for c in range(d_ff // ff_chunk):
        cols = slice(c * ff_chunk, (c + 1) * ff_chunk)
        a = jnp.square(jnp.maximum(_dot(hn, w1_ref[:, cols]), 0.0))
        h = h + _dot(a.astype(MXU_DTYPE), w2_ref[cols, :])
    hn = (h * _rms_scale(h) * pg_ref[...]).astype(MXU_DTYPE)
    gate = _sigmoid(_dot(hn, wg_ref[...]))
    emb = _dot(p_ref[...].astype(MXU_DTYPE), wp_ref[...])
    o_ref[...] = h + gate * emb


def _tail(h2d, y2d, p2d, w_out, mlp_gain, w1, w2, ple_gain, wg, wp, tm, ff_chunk):
    t, d = h2d.shape
    row = lambda i: (i, 0)
    return pl.pallas_call(
        functools.partial(_tail_kernel, ff_chunk=ff_chunk),
        grid=(t // tm,),
        in_specs=[
            pl.BlockSpec((tm, d), row),
            pl.BlockSpec((tm, y2d.shape[1]), row),
            pl.BlockSpec((tm, p2d.shape[1]), row),
            _resident(w_out.shape),
            _resident((1, d)),
            _resident(w1.shape),
            _resident(w2.shape),
            _resident((1, d)),
            _resident(wg.shape),
            _resident(wp.shape),
        ],
        out_specs=pl.BlockSpec((tm, d), row),
        out_shape=jax.ShapeDtypeStruct((t, d), F32),
        compiler_params=_params("parallel"),
        name="layer_tail",
    )(h2d, y2d, p2d, w_out.astype(MXU_DTYPE), mlp_gain.reshape(1, d), w1.astype(MXU_DTYPE),
      w2.astype(MXU_DTYPE), ple_gain.reshape(1, d), wg.astype(MXU_DTYPE), wp.astype(MXU_DTYPE))


def _mla_proj_kernel(x_ref, g_ref, win_ref, qa_ref, kva_ref, wuq_ref, wukv_ref, qn_ref, kn_ref,
                     qrope_ref, krope_ref, q_ref, k_ref, v_ref):
    x = x_ref[...]
    hn = (x * _rms_scale(x) * g_ref[...]).astype(MXU_DTYPE)
    proj = _dot(hn, win_ref[...])
    c_q = proj[:, :MLA_Q_RANK]
    c_kv = proj[:, MLA_Q_RANK:MLA_Q_RANK + MLA_KV_RANK]
    k_r = proj[:, MLA_Q_RANK + MLA_KV_RANK:]
    c_q = (c_q * _rms_scale(c_q) * qa_ref[...]).astype(MXU_DTYPE)
    c_kv = (c_kv * _rms_scale(c_kv) * kva_ref[...]).astype(MXU_DTYPE)

    lane = lax.broadcasted_iota(jnp.int32, (1, LANES), 1)
    first_half = (lane < MLA_ROPE).astype(F32)
    softmax_scale = MLA_QKD ** -0.5 * LOG2_E
    qn = qn_ref[...]
    kn = kn_ref[...]

    k_rot = k_r * krope_ref[...]
    k_rot = (k_rot + pltpu.roll(k_rot, MLA_ROPE, 1)) * first_half
    k_r_ss = 0.5 * jnp.sum(k_r * k_r, axis=-1, keepdims=True)

    for h in range(MLA_HEADS):
        lo = h * MLA_HEAD_W
        qh = _dot(c_q, wuq_ref[:, lo:lo + MLA_HEAD_W])
        q_nope, q_r = qh[:, :MLA_NOPE], qh[:, MLA_NOPE:]
        ss = jnp.sum(q_nope * q_nope, axis=-1, keepdims=True) + 0.5 * jnp.sum(q_r * q_r, axis=-1, keepdims=True)
        inv = lax.rsqrt(ss * (1.0 / MLA_QKD) + EPS) * softmax_scale
        q_rot = q_r * qrope_ref[...]
        q_rot = q_rot + pltpu.roll(q_rot, MLA_ROPE, 1)
        q_ref[:, lo:lo + MLA_NOPE] = (q_nope * inv * qn).astype(q_ref.dtype)
        q_ref[:, lo + MLA_NOPE:lo + MLA_HEAD_W] = (q_rot * inv).astype(q_ref.dtype)

        kvh = _dot(c_kv, wukv_ref[:, lo:lo + MLA_HEAD_W])
        k_nope, v = kvh[:, :MLA_NOPE], kvh[:, MLA_NOPE:]
        ss = jnp.sum(k_nope * k_nope, axis=-1, keepdims=True) + k_r_ss
        inv = lax.rsqrt(ss * (1.0 / MLA_QKD) + EPS)
        k_ref[:, lo:lo + MLA_NOPE] = (k_nope * inv * kn).astype(k_ref.dtype)
        k_ref[:, lo + MLA_NOPE:lo + MLA_HEAD_W] = (k_rot * inv).astype(k_ref.dtype)
        v_ref[:, h * MLA_VD:(h + 1) * MLA_VD] = v.astype(v_ref.dtype)


def _rotate_half_cols(w):
    half = w.shape[-1] // 2
    return jnp.concatenate([-w[..., half:], w[..., :half]], axis=-1)


def _rope_gain_table(seq, gain_rope):
    cos, sin = _rope_angles(seq, MLA_ROPE)
    cos = jnp.concatenate([cos, cos], axis=-1)
    sin = jnp.concatenate([sin, sin], axis=-1)
    half = MLA_ROPE // 2
    gain_rot = jnp.concatenate([gain_rope[half:], gain_rope[:half]])
    return jnp.concatenate([cos * gain_rope[None, :], sin * gain_rot[None, :]], axis=-1)


def _mla_proj(x2d, gain, w_in, q_a_gain, kv_a_gain, w_uq, w_ukv, q_gain, k_gain, seq, tm):
    t, d = x2d.shape
    kr_lo = MLA_Q_RANK + MLA_KV_RANK
    w_in_ext = jnp.concatenate([w_in, _rotate_half_cols(w_in[:, kr_lo:])], axis=1)
    w_uq_h = w_uq.reshape(MLA_Q_RANK, MLA_HEADS, MLA_QKD)
    w_uq_ext = jnp.concatenate([w_uq_h, _rotate_half_cols(w_uq_h[..., MLA_NOPE:])], axis=-1)
    w_uq_ext = w_uq_ext.reshape(MLA_Q_RANK, MLA_HEADS * MLA_HEAD_W)
    q_rope = _rope_gain_table(seq, q_gain[MLA_NOPE:])
    k_rope = _rope_gain_table(seq, k_gain[MLA_NOPE:])
    qk_w = MLA_HEADS * MLA_HEAD_W
    v_w = MLA_HEADS * MLA_VD
    pos_blocks = seq // tm
    row = lambda i: (i, 0)
    pos = lambda i: (i % pos_blocks, 0)
    return pl.pallas_call(
        _mla_proj_kernel,
        grid=(t // tm,),
        in_specs=[
            pl.BlockSpec((tm, d), row),
            _resident((1, d)),
            _resident(w_in_ext.shape),
            _resident((1, MLA_Q_RANK)),
            _resident((1, MLA_KV_RANK)),
            _resident(w_uq_ext.shape),
            _resident(w_ukv.shape),
            _resident((1, MLA_NOPE)),
            _resident((1, MLA_NOPE)),
            pl.BlockSpec((tm, LANES), pos),
            pl.BlockSpec((tm, LANES), pos),
        ],
        out_specs=[
            pl.BlockSpec((tm, qk_w), row),
            pl.BlockSpec((tm, qk_w), row),
            pl.BlockSpec((tm, v_w), row),
        ],
        out_shape=[
            jax.ShapeDtypeStruct((t, qk_w), MXU_DTYPE),
            jax.ShapeDtypeStruct((t, qk_w), MXU_DTYPE),
            jax.ShapeDtypeStruct((t, v_w), MXU_DTYPE),
        ],
        compiler_params=_params("parallel"),
        name="mla_proj",
    )(x2d, gain.reshape(1, d), w_in_ext.astype(MXU_DTYPE), q_a_gain.reshape(1, -1), kv_a_gain.reshape(1, -1),
      w_uq_ext.astype(MXU_DTYPE), w_ukv.astype(MXU_DTYPE), q_gain[:MLA_NOPE].reshape(1, -1),
      k_gain[:MLA_NOPE].reshape(1, -1), q_rope, k_rope)


def _attn_kernel(q_ref, k_ref, v_ref, o_ref, sa_ref, sb_ref, m_ref, acc_ref, *, block):
    qi = pl.program_id(2)
    q = q_ref[0]
    m_ref[...] = jnp.full_like(m_ref, -1e30)
    acc_ref[...] = jnp.zeros_like(acc_ref)
    ones = jnp.ones((block, LANES), MXU_DTYPE)

    def scores(start):
        return lax.dot_general(q, k_ref[0, pl.ds(start, block), :], _NT, preferred_element_type=F32)

    def update(s, start, masked):
        v = v_ref[0, pl.ds(start, block), :]
        if masked:
            q_chunk = lax.broadcasted_iota(jnp.int32, s.shape, 0) // CHUNK
            k_chunk = lax.broadcasted_iota(jnp.int32, s.shape, 1) // CHUNK
            s = jnp.where(k_chunk <= q_chunk, s, -1e30)
        m_prev = m_ref[...]
        m_new = jnp.maximum(m_prev, jnp.max(s, axis=-1, keepdims=True))
        alpha = jnp.exp2(m_prev - m_new)
        p = jnp.exp2(s - jnp.tile(m_new, (1, block // LANES)))
        pv = _dot(p.astype(MXU_DTYPE), jnp.concatenate([v, ones], axis=1))
        acc_ref[...] = jnp.tile(alpha, (1, 2)) * acc_ref[...] + pv
        m_ref[...] = m_new

    def at(j):
        return pl.multiple_of(j * block, block)

    sa_ref[...] = scores(0)

    def body(t, carry):
        sb_ref[...] = scores(at(2 * t + 1))
        update(sa_ref[...], at(2 * t), False)
        sa_ref[...] = scores(at(2 * t + 2))
        update(sb_ref[...], at(2 * t + 1), False)
        return carry

    lax.fori_loop(0, qi // 2, body, 0)

    @pl.when(qi % 2 == 0)
    def _():
        update(sa_ref[...], at(qi), True)

    @pl.when(qi % 2 == 1)
    def _():
        sb_ref[...] = scores(at(qi))
        update(sa_ref[...], at(qi - 1), False)
        update(sb_ref[...], at(qi), True)

    acc = acc_ref[...]
    o_ref[0] = (acc[:, :MLA_VD] / acc[:, MLA_VD:]).astype(o_ref.dtype)


def _attention(q, k, v, block):
    b, s, _ = q.shape
    return pl.pallas_call(
        functools.partial(_attn_kernel, block=block),
        grid=(b, MLA_HEADS, s // block),
        in_specs=[
            pl.BlockSpec((1, block, MLA_HEAD_W), lambda i, h, j: (i, j, h)),
            pl.BlockSpec((1, s, MLA_HEAD_W), lambda i, h, j: (i, 0, h)),
            pl.BlockSpec((1, s, MLA_VD), lambda i, h, j: (i, 0, h)),
        ],
        out_specs=pl.BlockSpec((1, block, MLA_VD), lambda i, h, j: (i, j, h)),
        out_shape=jax.ShapeDtypeStruct((b, s, MLA_HEADS * MLA_VD), MXU_DTYPE),
        scratch_shapes=[
            pltpu.VMEM((block, block), F32),
            pltpu.VMEM((block, block), F32),
            pltpu.VMEM((block, LANES), F32),
            pltpu.VMEM((block, MLA_VD + LANES), F32),
        ],
        compiler_params=_params("parallel", "parallel", "arbitrary"),
        name="mla_attention",
    )(q, k, v)


def _tiles(seq):
    assert seq % RET_SUPER == 0, seq
    tm = 512 if seq % 512 == 0 else RET_SUPER
    return dict(tm=tm, ret_block=tm, attn_block=tm, ff_chunk=1024)


def kernel(x, p, mix_norm, ret_w_in, ret_gn, ret_w_out, mla_w_in, mla_q_a_norm, mla_kv_a_norm, mla_w_uq,
           mla_w_ukv, mla_q_norm, mla_k_norm, mla_w_out, mlp_norm, mlp_w1, mlp_w2, ple_norm, ple_gate_w,
           ple_proj_w):
    b, s, d = x.shape
    depth = p.shape[0]
    t = b * s
    cfg = _tiles(s)
    tm = cfg["tm"]
    h = x.reshape(t, d)
    for i in range(depth):
        j = i // 2
        if i % 2 == 0:
            q, k, v, gate = _ret_proj(h, mix_norm[i], ret_w_in[j], s, tm)
            y = _retention(q.reshape(b, s, -1), k.reshape(b, s, -1), v.reshape(b, s, -1),
                           gate.reshape(b, s, -1), ret_gn[j], cfg["ret_block"])
            w_out = ret_w_out[j]
        else:
            q, k, v = _mla_proj(h, mix_norm[i], mla_w_in[j], mla_q_a_norm[j], mla_kv_a_norm[j], mla_w_uq[j],
                                mla_w_ukv[j], mla_q_norm[j], mla_k_norm[j], s, tm)
            y = _attention(q.reshape(b, s, -1), k.reshape(b, s, -1), v.reshape(b, s, -1), cfg["attn_block"])
            w_out = mla_w_out[j]
        h = _tail(h, y.reshape(t, -1), p[i].reshape(t, -1), w_out, mlp_norm[i], mlp_w1[i], mlp_w2[i],
                  ple_norm[i], ple_gate_w[i], ple_proj_w[i], tm, cfg["ff_chunk"])
    return h.reshape(b, s, d)
```

```python
import functools

import jax
import jax.numpy as jnp
from jax import lax
from jax.experimental import pallas as pl
from jax.experimental.pallas import tpu as pltpu

EPS = 1e-6
ROPE_THETA = 10000.0
CHUNK = 64
LANES = 128

RET_HEADS = 4
RET_DK = 256
RET_DV = 512
RET_SUPER = 256

MLA_HEADS = 8
MLA_NOPE = 128
MLA_ROPE = 64
MLA_QKD = MLA_NOPE + MLA_ROPE
MLA_VD = 128
MLA_Q_RANK = 384
MLA_KV_RANK = 256
MLA_HEAD_W = 256

MXU_DTYPE = jnp.bfloat16
F32 = jnp.float32
LOG2_E = 1.4426950408889634
VMEM_LIMIT = 56 * 1024 * 1024

_NT = (((1,), (1,)), ((), ()))
_TN = (((0,), (0,)), ((), ()))


def _dot(a, b):
    return jnp.dot(a, b, preferred_element_type=F32)


def _rms_scale(x):
    return lax.rsqrt(jnp.mean(x * x, axis=-1, keepdims=True) + EPS)


def _row_sumsq(x, weights=None):
    if weights is None:
        weights = jnp.ones((x.shape[-1], LANES), MXU_DTYPE)
    return _dot((x * x).astype(MXU_DTYPE), weights)


def _lanes(a, width):
    return a if width == LANES else jnp.tile(a, (1, width // LANES))


def _sigmoid(x):
    return 0.5 * jnp.tanh(0.5 * x) + 0.5


def _resident(shape):
    zeros = (0,) * len(shape)
    return pl.BlockSpec(shape, lambda *_: zeros, pipeline_mode=pl.Buffered(1))


def _params(*semantics):
    return pltpu.CompilerParams(dimension_semantics=semantics, vmem_limit_bytes=VMEM_LIMIT)


def _rope_angles(seq, dim):
    inv = 1.0 / (ROPE_THETA ** (jnp.arange(0, dim, 2, dtype=F32) / dim))
    ang = jnp.arange(seq, dtype=F32)[:, None] * inv[None, :]
    return jnp.cos(ang), jnp.sin(ang)


def _ret_proj_kernel(x_ref, g_ref, w_ref, cos_ref, sin_ref, gn_ref, q_ref, k_ref, v_ref, gate_ref):
    x = x_ref[...]
    hn = (x * _rms_scale(x) * g_ref[...]).astype(MXU_DTYPE)
    cos = cos_ref[...]
    sin = sin_ref[...]
    half = RET_DK // 2
    qk_w = RET_HEADS * RET_DK
    v_w = RET_HEADS * RET_DV
    k_scale = RET_DK ** -0.5
    for h in range(RET_HEADS):
        lo = h * RET_DV
        g = _dot(hn, w_ref[:, 2 * qk_w + v_w + lo:2 * qk_w + v_w + lo + RET_DV])
        gate_ref[:, lo:lo + RET_DV] = (g * _sigmoid(g) * gn_ref[:, lo:lo + RET_DV]).astype(gate_ref.dtype)
    for h in range(RET_HEADS):
        lo = h * RET_DK
        r = _dot(hn, w_ref[:, lo:lo + RET_DK])
        x1, x2 = r[:, :half], r[:, half:]
        q_ref[:, lo:lo + half] = (x1 * cos - x2 * sin).astype(q_ref.dtype)
        q_ref[:, lo + half:lo + RET_DK] = (x2 * cos + x1 * sin).astype(q_ref.dtype)
        r = _dot(hn, w_ref[:, qk_w + lo:qk_w + lo + RET_DK])
        x1, x2 = r[:, :half], r[:, half:]
        k_ref[:, lo:lo + half] = ((x1 * cos - x2 * sin) * k_scale).astype(k_ref.dtype)
        k_ref[:, lo + half:lo + RET_DK] = ((x2 * cos + x1 * sin) * k_scale).astype(k_ref.dtype)
    for h in range(RET_HEADS):
        lo = h * RET_DV
        v_ref[:, lo:lo + RET_DV] = _dot(hn, w_ref[:, 2 * qk_w + lo:2 * qk_w + lo + RET_DV]).astype(v_ref.dtype)


def _ret_proj(x2d, gain, w_in, gn_gain, seq, tm):
    t, d = x2d.shape
    qk_w = RET_HEADS * RET_DK
    v_w = RET_HEADS * RET_DV
    cos, sin = _rope_angles(seq, RET_DK)
    pos_blocks = seq // tm
    row = lambda i: (i, 0)
    pos = lambda i: (i % pos_blocks, 0)
    return pl.pallas_call(
        _ret_proj_kernel,
        grid=(t // tm,),
        in_specs=[
            pl.BlockSpec((tm, d), row),
            _resident((1, d)),
            _resident(w_in.shape),
            pl.BlockSpec((tm, RET_DK // 2), pos),
            pl.BlockSpec((tm, RET_DK // 2), pos),
            _resident((1, v_w)),
        ],
        out_specs=[
            pl.BlockSpec((tm, qk_w), row),
            pl.BlockSpec((tm, qk_w), row),
            pl.BlockSpec((tm, v_w), row),
            pl.BlockSpec((tm, v_w), row),
        ],
        out_shape=[
            jax.ShapeDtypeStruct((t, qk_w), MXU_DTYPE),
            jax.ShapeDtypeStruct((t, qk_w), MXU_DTYPE),
            jax.ShapeDtypeStruct((t, v_w), MXU_DTYPE),
            jax.ShapeDtypeStruct((t, v_w), MXU_DTYPE),
        ],
        compiler_params=_params("parallel"),
        name="ret_proj",
    )(x2d, gain.reshape(1, d), w_in.astype(MXU_DTYPE), cos, sin, gn_gain.reshape(1, v_w))


def _retention_kernel(q_ref, k_ref, v_ref, gate_ref, decay_ref, qd_ref, kd_ref, cd_ref, y_ref, state_ref,
                      *, groups):
    @pl.when(pl.program_id(1) == 0)
    def _():
        state_ref[...] = jnp.zeros_like(state_ref)

    for c in range(groups):
        rows = slice(c * RET_SUPER, (c + 1) * RET_SUPER)
        for h in range(RET_HEADS):
            qk_cols = slice(h * RET_DK, (h + 1) * RET_DK)
            v_cols = slice(h * RET_DV, (h + 1) * RET_DV)
            q = q_ref[0, rows, qk_cols]
            k = k_ref[0, rows, qk_cols]
            v = v_ref[0, rows, v_cols]
            state = state_ref[h]
            scores = lax.dot_general(q, k, _NT, preferred_element_type=F32) * decay_ref[h]
            inner = _dot(scores.astype(MXU_DTYPE), v)
            cross = _dot(q, state.astype(MXU_DTYPE)) * qd_ref[h]
            out = inner + cross
            k_dec = (k.astype(F32) * kd_ref[h]).astype(MXU_DTYPE)
            state_ref[h] = state * cd_ref[h] + lax.dot_general(k_dec, v, _TN, preferred_element_type=F32)
            gate = gate_ref[0, rows, v_cols].astype(F32)
            y_ref[0, rows, v_cols] = (gate * (out * _rms_scale(out))).astype(y_ref.dtype)


def _retention_tables():
    log_gamma = jnp.log(1.0 - 2.0 ** (-5.0 - jnp.arange(RET_HEADS, dtype=F32)))
    idx = jnp.arange(RET_SUPER, dtype=F32)
    dist = jnp.abs(idx[:, None] - idx[None, :])
    chunk = jnp.arange(RET_SUPER) // CHUNK
    visible = chunk[None, :] <= chunk[:, None]
    decay = jnp.where(visible[None], jnp.exp(log_gamma[:, None, None] * dist), 0.0)
    q_decay = jnp.exp(log_gamma[:, None] * (idx + 1.0))
    k_decay = jnp.exp(log_gamma[:, None] * (RET_SUPER - 1.0 - idx))
    group_decay = jnp.exp(log_gamma * RET_SUPER)
    qd = jnp.broadcast_to(q_decay[:, :, None], (RET_HEADS, RET_SUPER, RET_DV))
    kd = jnp.broadcast_to(k_decay[:, :, None], (RET_HEADS, RET_SUPER, RET_DK))
    cd = jnp.broadcast_to(group_decay[:, None, None], (RET_HEADS, 1, RET_DV))
    return decay.astype(F32), qd.astype(F32), kd.astype(F32), cd.astype(F32)


def _retention(q, k, v, gate, block):
    b, s, _ = q.shape
    qk_w = RET_HEADS * RET_DK
    v_w = RET_HEADS * RET_DV
    decay, qd, kd, cd = _retention_tables()
    tok = lambda i, j: (i, j, 0)
    return pl.pallas_call(
        functools.partial(_retention_kernel, groups=block // RET_SUPER),
        grid=(b, s // block),
        in_specs=[
            pl.BlockSpec((1, block, qk_w), tok),
            pl.BlockSpec((1, block, qk_w), tok),
            pl.BlockSpec((1, block, v_w), tok),
            pl.BlockSpec((1, block, v_w), tok),
            _resident(decay.shape),
            _resident(qd.shape),
            _resident(kd.shape),
            _resident(cd.shape),
        ],
        out_specs=pl.BlockSpec((1, block, v_w), tok),
        out_shape=jax.ShapeDtypeStruct((b, s, v_w), MXU_DTYPE),
        scratch_shapes=[pltpu.VMEM((RET_HEADS, RET_DK, RET_DV), F32)],
        compiler_params=_params("parallel", "arbitrary"),
        name="retention",
    )(q, k, v, gate, decay, qd, kd, cd)


def _tail_kernel(h_ref, y_ref, p_ref, wo_ref, mg_ref, w1_ref, w2_ref, pg_ref, wg_ref, wp_ref, o_ref,
                 *, ff_chunk):
    h = h_ref[...] + _dot(y_ref[...], wo_ref[...])
    hn = (h * _rms_scale(h) * mg_ref[...]).astype(MXU_DTYPE)
    d_ff = w1_ref.shape[1]
    for c in range(d_ff // ff_chunk):
        cols = slice(c * ff_chunk, (c + 1) * ff_chunk)
        a = jnp.square(jnp.maximum(_dot(hn, w1_ref[:, cols]), 0.0))
        h = h + _dot(a.astype(MXU_DTYPE), w2_ref[cols, :])
    hn = (h * _rms_scale(h) * pg_ref[...]).astype(MXU_DTYPE)
    gate = _sigmoid(_dot(hn, wg_ref[...]))
    emb = _dot(p_ref[...].astype(MXU_DTYPE), wp_ref[...])
    o_ref[...] = h + gate * emb


def _tail(h2d, y2d, p2d, w_out, mlp_gain, w1, w2, ple_gain, wg, wp, tm, ff_chunk):
    t, d = h2d.shape
    row = lambda i: (i, 0)
    return pl.pallas_call(
        functools.partial(_tail_kernel, ff_chunk=ff_chunk),
        grid=(t // tm,),
        in_specs=[
            pl.BlockSpec((tm, d), row),
            pl.BlockSpec((tm, y2d.shape[1]), row),
            pl.BlockSpec((tm, p2d.shape[1]), row),
            _resident(w_out.shape),
            _resident((1, d)),
            _resident(w1.shape),
            _resident(w2.shape),
            _resident((1, d)),
            _resident(wg.shape),
            _resident(wp.shape),
        ],
        out_specs=pl.BlockSpec((tm, d), row),
        out_shape=jax.ShapeDtypeStruct((t, d), F32),
        compiler_params=_params("parallel"),
        name="layer_tail",
    )(h2d, y2d, p2d, w_out.astype(MXU_DTYPE), mlp_gain.reshape(1, d), w1.astype(MXU_DTYPE),
      w2.astype(MXU_DTYPE), ple_gain.reshape(1, d), wg.astype(MXU_DTYPE), wp.astype(MXU_DTYPE))


def _mla_proj_kernel(x_ref, g_ref, win_ref, qa_ref, kva_ref, wuq_ref, wukv_ref, qn_ref, kn_ref,
                     qrope_ref, krope_ref, q_ref, k_ref, v_ref):
    x = x_ref[...]
    d = x.shape[-1]
    inv = lax.rsqrt(_row_sumsq(x) * (1.0 / d) + EPS)
    hn = (x * _lanes(inv, d) * g_ref[...]).astype(MXU_DTYPE)
    proj = _dot(hn, win_ref[...])
    c_q = proj[:, :MLA_Q_RANK]
    c_kv = proj[:, MLA_Q_RANK:MLA_Q_RANK + MLA_KV_RANK]
    k_r = proj[:, MLA_Q_RANK + MLA_KV_RANK:]
    inv = lax.rsqrt(_row_sumsq(c_q) * (1.0 / MLA_Q_RANK) + EPS)
    c_q = (c_q * _lanes(inv, MLA_Q_RANK) * qa_ref[...]).astype(MXU_DTYPE)
    inv = lax.rsqrt(_row_sumsq(c_kv) * (1.0 / MLA_KV_RANK) + EPS)
    c_kv = (c_kv * _lanes(inv, MLA_KV_RANK) * kva_ref[...]).astype(MXU_DTYPE)

    lane = lax.broadcasted_iota(jnp.int32, (1, LANES), 1)
    first_half = (lane < MLA_ROPE).astype(F32)
    softmax_scale = MLA_QKD ** -0.5 * LOG2_E
    qn = qn_ref[...]
    kn = kn_ref[...]
    ones = jnp.ones((MLA_NOPE, LANES), MXU_DTYPE)
    halves = jnp.full((LANES, LANES), 0.5, MXU_DTYPE)
    head_weights = jnp.concatenate([ones, halves], axis=0)

    k_rot = k_r * krope_ref[...]
    k_rot = (k_rot + pltpu.roll(k_rot, MLA_ROPE, 1)) * first_half
    k_r_ss = _row_sumsq(k_r, halves)

    for h in range(MLA_HEADS):
        lo = h * MLA_HEAD_W
        qh = _dot(c_q, wuq_ref[:, lo:lo + MLA_HEAD_W])
        q_nope, q_r = qh[:, :MLA_NOPE], qh[:, MLA_NOPE:]
        inv = lax.rsqrt(_row_sumsq(qh, head_weights) * (1.0 / MLA_QKD) + EPS) * softmax_scale
        q_rot = q_r * qrope_ref[...]
        q_rot = q_rot + pltpu.roll(q_rot, MLA_ROPE, 1)
        q_ref[:, lo:lo + MLA_NOPE] = (q_nope * inv * qn).astype(q_ref.dtype)
        q_ref[:, lo + MLA_NOPE:lo + MLA_HEAD_W] = (q_rot * inv).astype(q_ref.dtype)

        kvh = _dot(c_kv, wukv_ref[:, lo:lo + MLA_HEAD_W])
        k_nope, v = kvh[:, :MLA_NOPE], kvh[:, MLA_NOPE:]
        inv = lax.rsqrt((_row_sumsq(k_nope, ones) + k_r_ss) * (1.0 / MLA_QKD) + EPS)
        k_ref[:, lo:lo + MLA_NOPE] = (k_nope * inv * kn).astype(k_ref.dtype)
        k_ref[:, lo + MLA_NOPE:lo + MLA_HEAD_W] = (k_rot * inv).astype(k_ref.dtype)
        v_ref[:, h * MLA_VD:(h + 1) * MLA_VD] = v.astype(v_ref.dtype)


def _rotate_half_cols(w):
    half = w.shape[-1] // 2
    return jnp.concatenate([-w[..., half:], w[..., :half]], axis=-1)


def _rope_gain_table(seq, gain_rope):
    cos, sin = _rope_angles(seq, MLA_ROPE)
    cos = jnp.concatenate([cos, cos], axis=-1)
    sin = jnp.concatenate([sin, sin], axis=-1)
    half = MLA_ROPE // 2
    gain_rot = jnp.concatenate([gain_rope[half:], gain_rope[:half]])
    return jnp.concatenate([cos * gain_rope[None, :], sin * gain_rot[None, :]], axis=-1)


def _mla_proj(x2d, gain, w_in, q_a_gain, kv_a_gain, w_uq, w_ukv, q_gain, k_gain, seq, tm):
    t, d = x2d.shape
    kr_lo = MLA_Q_RANK + MLA_KV_RANK
    w_in_ext = jnp.concatenate([w_in, _rotate_half_cols(w_in[:, kr_lo:])], axis=1)
    w_uq_h = w_uq.reshape(MLA_Q_RANK, MLA_HEADS, MLA_QKD)
    w_uq_ext = jnp.concatenate([w_uq_h, _rotate_half_cols(w_uq_h[..., MLA_NOPE:])], axis=-1)
    w_uq_ext = w_uq_ext.reshape(MLA_Q_RANK, MLA_HEADS * MLA_HEAD_W)
    q_rope = _rope_gain_table(seq, q_gain[MLA_NOPE:])
    k_rope = _rope_gain_table(seq, k_gain[MLA_NOPE:])
    qk_w = MLA_HEADS * MLA_HEAD_W
    v_w = MLA_HEADS * MLA_VD
    pos_blocks = seq // tm
    row = lambda i: (i, 0)
    pos = lambda i: (i % pos_blocks, 0)
    return pl.pallas_call(
        _mla_proj_kernel,
        grid=(t // tm,),
        in_specs=[
            pl.BlockSpec((tm, d), row),
            _resident((1, d)),
            _resident(w_in_ext.shape),
            _resident((1, MLA_Q_RANK)),
            _resident((1, MLA_KV_RANK)),
            _resident(w_uq_ext.shape),
            _resident(w_ukv.shape),
            _resident((1, MLA_NOPE)),
            _resident((1, MLA_NOPE)),
            pl.BlockSpec((tm, LANES), pos),
            pl.BlockSpec((tm, LANES), pos),
        ],
        out_specs=[
            pl.BlockSpec((tm, qk_w), row),
            pl.BlockSpec((tm, qk_w), row),
            pl.BlockSpec((tm, v_w), row),
        ],
        out_shape=[
            jax.ShapeDtypeStruct((t, qk_w), MXU_DTYPE),
            jax.ShapeDtypeStruct((t, qk_w), MXU_DTYPE),
            jax.ShapeDtypeStruct((t, v_w), MXU_DTYPE),
        ],
        compiler_params=_params("parallel"),
        name="mla_proj",
    )(x2d, gain.reshape(1, d), w_in_ext.astype(MXU_DTYPE), q_a_gain.reshape(1, -1), kv_a_gain.reshape(1, -1),
      w_uq_ext.astype(MXU_DTYPE), w_ukv.astype(MXU_DTYPE), q_gain[:MLA_NOPE].reshape(1, -1),
      k_gain[:MLA_NOPE].reshape(1, -1), q_rope, k_rope)


def _attn_kernel(q_ref, k_ref, v_ref, o_ref, sa_ref, sb_ref, m_ref, acc_ref, *, kblock):
    qi = pl.program_id(2)
    m_ref[...] = jnp.full_like(m_ref, -1e30)
    acc_ref[...] = jnp.zeros_like(acc_ref)
    ones = jnp.ones((kblock, LANES), MXU_DTYPE)
    all_rows = slice(0, 2 * kblock)
    low_rows = slice(kblock, 2 * kblock)

    def at(j):
        return pl.multiple_of(j * kblock, kblock)

    def scores(rows, j):
        return lax.dot_general(q_ref[0, rows, :], k_ref[0, pl.ds(at(j), kblock), :], _NT,
                               preferred_element_type=F32)

    def update(s, rows, j, masked):
        v = v_ref[0, pl.ds(at(j), kblock), :]
        if masked:
            q_chunk = lax.broadcasted_iota(jnp.int32, s.shape, 0) // CHUNK
            k_chunk = lax.broadcasted_iota(jnp.int32, s.shape, 1) // CHUNK
            s = jnp.where(k_chunk <= q_chunk, s, -1e30)
        m_prev = m_ref[rows, :]
        m_new = jnp.maximum(m_prev, jnp.max(s, axis=-1, keepdims=True))
        alpha = jnp.exp2(m_prev - m_new)
        p = jnp.exp2(s - jnp.tile(m_new, (1, kblock // LANES)))
        pv = _dot(p.astype(MXU_DTYPE), jnp.concatenate([v, ones], axis=1))
        acc_ref[rows, :] = jnp.tile(alpha, (1, 2)) * acc_ref[rows, :] + pv
        m_ref[rows, :] = m_new

    sa_ref[...] = scores(all_rows, 0)

    def body(t, carry):
        sb_ref[...] = scores(all_rows, 2 * t + 1)
        update(sa_ref[...], all_rows, 2 * t, False)
        sa_ref[...] = scores(all_rows, 2 * t + 2)
        update(sb_ref[...], all_rows, 2 * t + 1, False)
        return carry

    lax.fori_loop(0, qi, body, 0)

    sb_ref[low_rows, :] = scores(low_rows, 2 * qi + 1)
    update(sa_ref[...], all_rows, 2 * qi, True)
    update(sb_ref[low_rows, :], low_rows, 2 * qi + 1, True)
    acc = acc_ref[...]
    o_ref[0] = (acc[:, :MLA_VD] / acc[:, MLA_VD:]).astype(o_ref.dtype)


def _attention(q, k, v, kblock):
    b, s, _ = q.shape
    qblock = 2 * kblock
    return pl.pallas_call(
        functools.partial(_attn_kernel, kblock=kblock),
        grid=(b, MLA_HEADS, s // qblock),
        in_specs=[
            pl.BlockSpec((1, qblock, MLA_HEAD_W), lambda i, h, j: (i, j, h)),
            pl.BlockSpec((1, s, MLA_HEAD_W), lambda i, h, j: (i, 0, h)),
            pl.BlockSpec((1, s, MLA_VD), lambda i, h, j: (i, 0, h)),
        ],
        out_specs=pl.BlockSpec((1, qblock, MLA_VD), lambda i, h, j: (i, j, h)),
        out_shape=jax.ShapeDtypeStruct((b, s, MLA_HEADS * MLA_VD), MXU_DTYPE),
        scratch_shapes=[
            pltpu.VMEM((qblock, kblock), F32),
            pltpu.VMEM((qblock, kblock), F32),
            pltpu.VMEM((qblock, LANES), F32),
            pltpu.VMEM((qblock, MLA_VD + LANES), F32),
        ],
        compiler_params=_params("parallel", "parallel", "arbitrary"),
        name="mla_attention",
    )(q, k, v)


def _tiles(seq):
    assert seq % RET_SUPER == 0, seq
    tm = 512 if seq % 512 == 0 else RET_SUPER
    attn_kblock = 512 if seq % 1024 == 0 else RET_SUPER // 2
    return dict(tm=tm, ret_block=tm, attn_kblock=attn_kblock, ff_chunk=1024)


def kernel(x, p, mix_norm, ret_w_in, ret_gn, ret_w_out, mla_w_in, mla_q_a_norm, mla_kv_a_norm, mla_w_uq,
           mla_w_ukv, mla_q_norm, mla_k_norm, mla_w_out, mlp_norm, mlp_w1, mlp_w2, ple_norm, ple_gate_w,
           ple_proj_w):
    b, s, d = x.shape
    depth = p.shape[0]
    t = b * s
    cfg = _tiles(s)
    tm = cfg["tm"]
    h = x.reshape(t, d)
    for i in range(depth):
        j = i // 2
        if i % 2 == 0:
            q, k, v, gate = _ret_proj(h, mix_norm[i], ret_w_in[j], ret_gn[j], s, tm)
            y = _retention(q.reshape(b, s, -1), k.reshape(b, s, -1), v.reshape(b, s, -1),
                           gate.reshape(b, s, -1), cfg["ret_block"])
            w_out = ret_w_out[j]
        else:
            q, k, v = _mla_proj(h, mix_norm[i], mla_w_in[j], mla_q_a_norm[j], mla_kv_a_norm[j], mla_w_uq[j],
                                mla_w_ukv[j], mla_q_norm[j], mla_k_norm[j], s, tm)
            y = _attention(q.reshape(b, s, -1), k.reshape(b, s, -1), v.reshape(b, s, -1), cfg["attn_kblock"])
            w_out = mla_w_out[j]
        h = _tail(h, y.reshape(t, -1), p[i].reshape(t, -1), w_out, mlp_norm[i], mlp_w1[i], mlp_w2[i],
                  ple_norm[i], ple_gate_w[i], ple_proj_w[i], tm, cfg["ff_chunk"])
    return h.reshape(b, s, d)
```

```python
import functools

import jax
import jax.numpy as jnp
from jax import lax
from jax.experimental import pallas as pl
from jax.experimental.pallas import tpu as pltpu

EPS = 1e-6
ROPE_THETA = 10000.0
CHUNK = 64
LANES = 128

RET_HEADS = 4
RET_DK = 256
RET_DV = 512
RET_SUPER = 256

MLA_HEADS = 8
MLA_NOPE = 128
MLA_ROPE = 64
MLA_QKD = MLA_NOPE + MLA_ROPE
MLA_VD = 128
MLA_Q_RANK = 384
MLA_KV_RANK = 256
MLA_HEAD_W = 256

MXU_DTYPE = jnp.bfloat16
F32 = jnp.float32
LOG2_E = 1.4426950408889634
VMEM_LIMIT = 56 * 1024 * 1024

_NT = (((1,), (1,)), ((), ()))
_TN = (((0,), (0,)), ((), ()))


def _dot(a, b):
    return jnp.dot(a, b, preferred_element_type=F32)


def _rms_scale(x):
    return lax.rsqrt(jnp.mean(x * x, axis=-1, keepdims=True) + EPS)


def _row_sumsq(x, weights=None):
    if weights is None:
        weights = jnp.ones((x.shape[-1], LANES), MXU_DTYPE)
    return _dot((x * x).astype(MXU_DTYPE), weights)


def _lanes(a, width):
    return a if width == LANES else jnp.tile(a, (1, width // LANES))


def _sigmoid(x):
    return 0.5 * jnp.tanh(0.5 * x) + 0.5


def _resident(shape):
    zeros = (0,) * len(shape)
    return pl.BlockSpec(shape, lambda *_: zeros, pipeline_mode=pl.Buffered(1))


def _params(*semantics):
    return pltpu.CompilerParams(dimension_semantics=semantics, vmem_limit_bytes=VMEM_LIMIT)


def _rope_angles(seq, dim):
    inv = 1.0 / (ROPE_THETA ** (jnp.arange(0, dim, 2, dtype=F32) / dim))
    ang = jnp.arange(seq, dtype=F32)[:, None] * inv[None, :]
    return jnp.cos(ang), jnp.sin(ang)


def _ret_proj_kernel(x_ref, g_ref, w_ref, cos_ref, sin_ref, gn_ref, q_ref, k_ref, v_ref, gate_ref):
    x = x_ref[...]
    hn = (x * _rms_scale(x) * g_ref[...]).astype(MXU_DTYPE)
    cos = cos_ref[...]
    sin = sin_ref[...]
    half = RET_DK // 2
    qk_w = RET_HEADS * RET_DK
    v_w = RET_HEADS * RET_DV
    k_scale = RET_DK ** -0.5
    for h in range(RET_HEADS):
        lo = h * RET_DV
        g = _dot(hn, w_ref[:, 2 * qk_w + v_w + lo:2 * qk_w + v_w + lo + RET_DV])
        gate_ref[:, lo:lo + RET_DV] = (g * _sigmoid(g) * gn_ref[:, lo:lo + RET_DV]).astype(gate_ref.dtype)
    for h in range(RET_HEADS):
        lo = h * RET_DK
        r = _dot(hn, w_ref[:, lo:lo + RET_DK])
        x1, x2 = r[:, :half], r[:, half:]
        q_ref[:, lo:lo + half] = (x1 * cos - x2 * sin).astype(q_ref.dtype)
        q_ref[:, lo + half:lo + RET_DK] = (x2 * cos + x1 * sin).astype(q_ref.dtype)
        r = _dot(hn, w_ref[:, qk_w + lo:qk_w + lo + RET_DK])
        x1, x2 = r[:, :half], r[:, half:]
        k_ref[:, lo:lo + half] = ((x1 * cos - x2 * sin) * k_scale).astype(k_ref.dtype)
        k_ref[:, lo + half:lo + RET_DK] = ((x2 * cos + x1 * sin) * k_scale).astype(k_ref.dtype)
    for h in range(RET_HEADS):
        lo = h * RET_DV
        v_ref[:, lo:lo + RET_DV] = _dot(hn, w_ref[:, 2 * qk_w + lo:2 * qk_w + lo + RET_DV]).astype(v_ref.dtype)


def _ret_proj(x2d, gain, w_in, gn_gain, seq, tm):
    t, d = x2d.shape
    qk_w = RET_HEADS * RET_DK
    v_w = RET_HEADS * RET_DV
    cos, sin = _rope_angles(seq, RET_DK)
    pos_blocks = seq // tm
    row = lambda i: (i, 0)
    pos = lambda i: (i % pos_blocks, 0)
    return pl.pallas_call(
        _ret_proj_kernel,
        grid=(t // tm,),
        in_specs=[
            pl.BlockSpec((tm, d), row),
            _resident((1, d)),
            _resident(w_in.shape),
            pl.BlockSpec((tm, RET_DK // 2), pos),
            pl.BlockSpec((tm, RET_DK // 2), pos),
            _resident((1, v_w)),
        ],
        out_specs=[
            pl.BlockSpec((tm, qk_w), row),
            pl.BlockSpec((tm, qk_w), row),
            pl.BlockSpec((tm, v_w), row),
            pl.BlockSpec((tm, v_w), row),
        ],
        out_shape=[
            jax.ShapeDtypeStruct((t, qk_w), MXU_DTYPE),
            jax.ShapeDtypeStruct((t, qk_w), MXU_DTYPE),
            jax.ShapeDtypeStruct((t, v_w), MXU_DTYPE),
            jax.ShapeDtypeStruct((t, v_w), MXU_DTYPE),
        ],
        compiler_params=_params("parallel"),
        name="ret_proj",
    )(x2d, gain.reshape(1, d), w_in.astype(MXU_DTYPE), cos, sin, gn_gain.reshape(1, v_w))


def _retention_kernel(q_ref, k_ref, v_ref, gate_ref, decay_ref, qd_ref, kd_ref, cd_ref, y_ref, state_ref,
                      *, groups):
    @pl.when(pl.program_id(1) == 0)
    def _():
        state_ref[...] = jnp.zeros_like(state_ref)

    for c in range(groups):
        rows = slice(c * RET_SUPER, (c + 1) * RET_SUPER)
        for h in range(RET_HEADS):
            qk_cols = slice(h * RET_DK, (h + 1) * RET_DK)
            v_cols = slice(h * RET_DV, (h + 1) * RET_DV)
            q = q_ref[0, rows, qk_cols]
            k = k_ref[0, rows, qk_cols]
            v = v_ref[0, rows, v_cols]
            state = state_ref[h]
            scores = lax.dot_general(q, k, _NT, preferred_element_type=F32) * decay_ref[h]
            inner = _dot(scores.astype(MXU_DTYPE), v)
            cross = _dot(q, state.astype(MXU_DTYPE)) * qd_ref[h]
            out = inner + cross
            k_dec = (k.astype(F32) * kd_ref[h]).astype(MXU_DTYPE)
            state_ref[h] = state * cd_ref[h] + lax.dot_general(k_dec, v, _TN, preferred_element_type=F32)
            gate = gate_ref[0, rows, v_cols].astype(F32)
            y_ref[0, rows, v_cols] = (gate * (out * _rms_scale(out))).astype(y_ref.dtype)


def _retention_tables():
    log_gamma = jnp.log(1.0 - 2.0 ** (-5.0 - jnp.arange(RET_HEADS, dtype=F32)))
    idx = jnp.arange(RET_SUPER, dtype=F32)
    dist = jnp.abs(idx[:, None] - idx[None, :])
    chunk = jnp.arange(RET_SUPER) // CHUNK
    visible = chunk[None, :] <= chunk[:, None]
    decay = jnp.where(visible[None], jnp.exp(log_gamma[:, None, None] * dist), 0.0)
    q_decay = jnp.exp(log_gamma[:, None] * (idx + 1.0))
    k_decay = jnp.exp(log_gamma[:, None] * (RET_SUPER - 1.0 - idx))
    group_decay = jnp.exp(log_gamma * RET_SUPER)
    qd = jnp.broadcast_to(q_decay[:, :, None], (RET_HEADS, RET_SUPER, RET_DV))
    kd = jnp.broadcast_to(k_decay[:, :, None], (RET_HEADS, RET_SUPER, RET_DK))
    cd = jnp.broadcast_to(group_decay[:, None, None], (RET_HEADS, 1, RET_DV))
    return decay.astype(F32), qd.astype(F32), kd.astype(F32), cd.astype(F32)


def _retention(q, k, v, gate, block):
    b, s, _ = q.shape
    qk_w = RET_HEADS * RET_DK
    v_w = RET_HEADS * RET_DV
    decay, qd, kd, cd = _retention_tables()
    tok = lambda i, j: (i, j, 0)
    return pl.pallas_call(
        functools.partial(_retention_kernel, groups=block // RET_SUPER),
        grid=(b, s // block),
        in_specs=[
            pl.BlockSpec((1, block, qk_w), tok),
            pl.BlockSpec((1, block, qk_w), tok),
            pl.BlockSpec((1, block, v_w), tok),
            pl.BlockSpec((1, block, v_w), tok),
            _resident(decay.shape),
            _resident(qd.shape),
            _resident(kd.shape),
            _resident(cd.shape),
        ],
        out_specs=pl.BlockSpec((1, block, v_w), tok),
        out_shape=jax.ShapeDtypeStruct((b, s, v_w), MXU_DTYPE),
        scratch_shapes=[pltpu.VMEM((RET_HEADS, RET_DK, RET_DV), F32)],
        compiler_params=_params("parallel", "arbitrary"),
        name="retention",
    )(q, k, v, gate, decay, qd, kd, cd)


def _tail_kernel(h_ref, y_ref, p_ref, wo_ref, mg_ref, w1_ref, w2_ref, pg_ref, wg_ref, wp_ref, o_ref,
                 *, ff_chunk):
    h = h_ref[...] + _dot(y_ref[...], wo_ref[...])
    hn = (h * _rms_scale(h) * mg_ref[...]).astype(MXU_DTYPE)
    d_ff = w1_ref.shape[1]
    for c in range(d_ff // ff_chunk):
        cols = slice(c * ff_chunk, (c + 1) * ff_chunk)
        a = jnp.square(jnp.maximum(_dot(hn, w1_ref[:, cols]), 0.0))
        h = h + _dot(a.astype(MXU_DTYPE), w2_ref[cols, :])
    hn = (h * _rms_scale(h) * pg_ref[...]).astype(MXU_DTYPE)
    gate = _sigmoid(_dot(hn, wg_ref[...]))
    emb = _dot(p_ref[...].astype(MXU_DTYPE), wp_ref[...])
    o_ref[...] = h + gate * emb


def _tail(h2d, y2d, p2d, w_out, mlp_gain, w1, w2, ple_gain, wg, wp, tm, ff_chunk):
    t, d = h2d.shape
    row = lambda i: (i, 0)
    return pl.pallas_call(
        functools.partial(_tail_kernel, ff_chunk=ff_chunk),
        grid=(t // tm,),
        in_specs=[
            pl.BlockSpec((tm, d), row),
            pl.BlockSpec((tm, y2d.shape[1]), row),
            pl.BlockSpec((tm, p2d.shape[1]), row),
            _resident(w_out.shape),
            _resident((1, d)),
            _resident(w1.shape),
            _resident(w2.shape),
            _resident((1, d)),
            _resident(wg.shape),
            _resident(wp.shape),
        ],
        out_specs=pl.BlockSpec((tm, d), row),
        out_shape=jax.ShapeDtypeStruct((t, d), F32),
        compiler_params=_params("parallel"),
        name="layer_tail",
    )(h2d, y2d, p2d, w_out.astype(MXU_DTYPE), mlp_gain.reshape(1, d), w1.astype(MXU_DTYPE),
      w2.astype(MXU_DTYPE), ple_gain.reshape(1, d), wg.astype(MXU_DTYPE), wp.astype(MXU_DTYPE))


def _mla_proj_kernel(x_ref, g_ref, win_ref, qa_ref, kva_ref, wuq_ref, wukv_ref, qn_ref, kn_ref,
                     qrope_ref, krope_ref, q_ref, k_ref, v_ref):
    x = x_ref[...]
    d = x.shape[-1]
    inv = lax.rsqrt(_row_sumsq(x) * (1.0 / d) + EPS)
    hn = (x * _lanes(inv, d) * g_ref[...]).astype(MXU_DTYPE)
    proj = _dot(hn, win_ref[...])
    c_q = proj[:, :MLA_Q_RANK]
    c_kv = proj[:, MLA_Q_RANK:MLA_Q_RANK + MLA_KV_RANK]
    k_r = proj[:, MLA_Q_RANK + MLA_KV_RANK:]
    inv = lax.rsqrt(_row_sumsq(c_q) * (1.0 / MLA_Q_RANK) + EPS)
    c_q = (c_q * _lanes(inv, MLA_Q_RANK) * qa_ref[...]).astype(MXU_DTYPE)
    inv = lax.rsqrt(_row_sumsq(c_kv) * (1.0 / MLA_KV_RANK) + EPS)
    c_kv = (c_kv * _lanes(inv, MLA_KV_RANK) * kva_ref[...]).astype(MXU_DTYPE)

    lane = lax.broadcasted_iota(jnp.int32, (1, LANES), 1)
    first_half = (lane < MLA_ROPE).astype(F32)
    softmax_scale = MLA_QKD ** -0.5 * LOG2_E
    qn = qn_ref[...]
    kn = kn_ref[...]
    ones = jnp.ones((MLA_NOPE, LANES), MXU_DTYPE)
    halves = jnp.full((LANES, LANES), 0.5, MXU_DTYPE)
    head_weights = jnp.concatenate([ones, halves], axis=0)

    k_rot = k_r * krope_ref[...]
    k_rot = (k_rot + pltpu.roll(k_rot, MLA_ROPE, 1)) * first_half
    k_r_ss = _row_sumsq(k_r, halves)

    def up_project(h):
        cols = slice(h * MLA_HEAD_W, (h + 1) * MLA_HEAD_W)
        return _dot(c_q, wuq_ref[:, cols]), _dot(c_kv, wukv_ref[:, cols])

    ahead = up_project(0)
    for h in range(MLA_HEADS):
        lo = h * MLA_HEAD_W
        qh, kvh = ahead
        if h + 1 < MLA_HEADS:
            ahead = up_project(h + 1)
        q_nope, q_r = qh[:, :MLA_NOPE], qh[:, MLA_NOPE:]
        inv = lax.rsqrt(_row_sumsq(qh, head_weights) * (1.0 / MLA_QKD) + EPS) * softmax_scale
        q_rot = q_r * qrope_ref[...]
        q_rot = q_rot + pltpu.roll(q_rot, MLA_ROPE, 1)
        q_ref[:, lo:lo + MLA_NOPE] = (q_nope * inv * qn).astype(q_ref.dtype)
        q_ref[:, lo + MLA_NOPE:lo + MLA_HEAD_W] = (q_rot * inv).astype(q_ref.dtype)

        k_nope, v = kvh[:, :MLA_NOPE], kvh[:, MLA_NOPE:]
        inv = lax.rsqrt((_row_sumsq(k_nope, ones) + k_r_ss) * (1.0 / MLA_QKD) + EPS)
        k_ref[:, lo:lo + MLA_NOPE] = (k_nope * inv * kn).astype(k_ref.dtype)
        k_ref[:, lo + MLA_NOPE:lo + MLA_HEAD_W] = (k_rot * inv).astype(k_ref.dtype)
        v_ref[:, h * MLA_VD:(h + 1) * MLA_VD] = v.astype(v_ref.dtype)


def _rotate_half_cols(w):
    half = w.shape[-1] // 2
    return jnp.concatenate([-w[..., half:], w[..., :half]], axis=-1)


def _rope_gain_table(seq, gain_rope):
    cos, sin = _rope_angles(seq, MLA_ROPE)
    cos = jnp.concatenate([cos, cos], axis=-1)
    sin = jnp.concatenate([sin, sin], axis=-1)
    half = MLA_ROPE // 2
    gain_rot = jnp.concatenate([gain_rope[half:], gain_rope[:half]])
    return jnp.concatenate([cos * gain_rope[None, :], sin * gain_rot[None, :]], axis=-1)


def _mla_proj(x2d, gain, w_in, q_a_gain, kv_a_gain, w_uq, w_ukv, q_gain, k_gain, seq, tm):
    t, d = x2d.shape
    kr_lo = MLA_Q_RANK + MLA_KV_RANK
    w_in_ext = jnp.concatenate([w_in, _rotate_half_cols(w_in[:, kr_lo:])], axis=1)
    w_uq_h = w_uq.reshape(MLA_Q_RANK, MLA_HEADS, MLA_QKD)
    w_uq_ext = jnp.concatenate([w_uq_h, _rotate_half_cols(w_uq_h[..., MLA_NOPE:])], axis=-1)
    w_uq_ext = w_uq_ext.reshape(MLA_Q_RANK, MLA_HEADS * MLA_HEAD_W)
    q_rope = _rope_gain_table(seq, q_gain[MLA_NOPE:])
    k_rope = _rope_gain_table(seq, k_gain[MLA_NOPE:])
    qk_w = MLA_HEADS * MLA_HEAD_W
    v_w = MLA_HEADS * MLA_VD
    pos_blocks = seq // tm
    row = lambda i: (i, 0)
    pos = lambda i: (i % pos_blocks, 0)
    return pl.pallas_call(
        _mla_proj_kernel,
        grid=(t // tm,),
        in_specs=[
            pl.BlockSpec((tm, d), row),
            _resident((1, d)),
            _resident(w_in_ext.shape),
            _resident((1, MLA_Q_RANK)),
            _resident((1, MLA_KV_RANK)),
            _resident(w_uq_ext.shape),
            _resident(w_ukv.shape),
            _resident((1, MLA_NOPE)),
            _resident((1, MLA_NOPE)),
            pl.BlockSpec((tm, LANES), pos),
            pl.BlockSpec((tm, LANES), pos),
        ],
        out_specs=[
            pl.BlockSpec((tm, qk_w), row),
            pl.BlockSpec((tm, qk_w), row),
            pl.BlockSpec((tm, v_w), row),
        ],
        out_shape=[
            jax.ShapeDtypeStruct((t, qk_w), MXU_DTYPE),
            jax.ShapeDtypeStruct((t, qk_w), MXU_DTYPE),
            jax.ShapeDtypeStruct((t, v_w), MXU_DTYPE),
        ],
        compiler_params=_params("parallel"),
        name="mla_proj",
    )(x2d, gain.reshape(1, d), w_in_ext.astype(MXU_DTYPE), q_a_gain.reshape(1, -1), kv_a_gain.reshape(1, -1),
      w_uq_ext.astype(MXU_DTYPE), w_ukv.astype(MXU_DTYPE), q_gain[:MLA_NOPE].reshape(1, -1),
      k_gain[:MLA_NOPE].reshape(1, -1), q_rope, k_rope)


def _attn_kernel(q_ref, k_ref, v_ref, o_ref, m_ref, acc_ref, *, block):
    n_blocks = q_ref.shape[1] // block
    ones = jnp.ones((block, LANES), MXU_DTYPE)
    pairs = [(qb, kb) for qb in range(n_blocks) for kb in range(qb + 1)]

    def rows(i):
        return slice(i * block, (i + 1) * block)

    def scores(qb, kb):
        return lax.dot_general(q_ref[0, rows(qb), :], k_ref[0, rows(kb), :], _NT, preferred_element_type=F32)

    def update(s, qb, kb):
        slot = qb % 2
        if kb == qb:
            q_chunk = lax.broadcasted_iota(jnp.int32, s.shape, 0) // CHUNK
            k_chunk = lax.broadcasted_iota(jnp.int32, s.shape, 1) // CHUNK
            s = jnp.where(k_chunk <= q_chunk, s, -1e30)
        m_new = jnp.broadcast_to(jnp.max(s, axis=-1, keepdims=True), (block, LANES))
        if kb > 0:
            m_prev = m_ref[slot]
            m_new = jnp.maximum(m_prev, m_new)
        p = jnp.exp2(s - jnp.tile(m_new, (1, block // LANES)))
        acc = _dot(p.astype(MXU_DTYPE), jnp.concatenate([v_ref[0, rows(kb), :], ones], axis=1))
        if kb > 0:
            acc = jnp.tile(jnp.exp2(m_prev - m_new), (1, 2)) * acc_ref[slot] + acc
        if kb == qb:
            o_ref[0, rows(qb), :] = (acc[:, :MLA_VD] / acc[:, MLA_VD:]).astype(o_ref.dtype)
        else:
            m_ref[slot] = m_new
            acc_ref[slot] = acc

    s_next = scores(*pairs[0])
    for u, (qb, kb) in enumerate(pairs):
        s = s_next
        if u + 1 < len(pairs):
            s_next = scores(*pairs[u + 1])
        update(s, qb, kb)


def _attention(q, k, v, block):
    b, s, _ = q.shape
    return pl.pallas_call(
        functools.partial(_attn_kernel, block=block),
        grid=(b, MLA_HEADS),
        in_specs=[
            pl.BlockSpec((1, s, MLA_HEAD_W), lambda i, h: (i, 0, h)),
            pl.BlockSpec((1, s, MLA_HEAD_W), lambda i, h: (i, 0, h)),
            pl.BlockSpec((1, s, MLA_VD), lambda i, h: (i, 0, h)),
        ],
        out_specs=pl.BlockSpec((1, s, MLA_VD), lambda i, h: (i, 0, h)),
        out_shape=jax.ShapeDtypeStruct((b, s, MLA_HEADS * MLA_VD), MXU_DTYPE),
        scratch_shapes=[
            pltpu.VMEM((2, block, LANES), F32),
            pltpu.VMEM((2, block, MLA_VD + LANES), F32),
        ],
        compiler_params=_params("parallel", "parallel"),
        name="mla_attention",
    )(q, k, v)


def _tiles(seq):
    assert seq % RET_SUPER == 0, seq
    tm = 512 if seq % 512 == 0 else RET_SUPER
    return dict(tm=tm, ret_block=tm, attn_block=tm, ff_chunk=1024)


def kernel(x, p, mix_norm, ret_w_in, ret_gn, ret_w_out, mla_w_in, mla_q_a_norm, mla_kv_a_norm, mla_w_uq,
           mla_w_ukv, mla_q_norm, mla_k_norm, mla_w_out, mlp_norm, mlp_w1, mlp_w2, ple_norm, ple_gate_w,
           ple_proj_w):
    b, s, d = x.shape
    depth = p.shape[0]
    t = b * s
    cfg = _tiles(s)
    tm = cfg["tm"]
    h = x.reshape(t, d)
    for i in range(depth):
        j = i // 2
        if i % 2 == 0:
            q, k, v, gate = _ret_proj(h, mix_norm[i], ret_w_in[j], ret_gn[j], s, tm)
            y = _retention(q.reshape(b, s, -1), k.reshape(b, s, -1), v.reshape(b, s, -1),
                           gate.reshape(b, s, -1), cfg["ret_block"])
            w_out = ret_w_out[j]
        else:
            q, k, v = _mla_proj(h, mix_norm[i], mla_w_in[j], mla_q_a_norm[j], mla_kv_a_norm[j], mla_w_uq[j],
                                mla_w_ukv[j], mla_q_norm[j], mla_k_norm[j], s, tm)
            y = _attention(q.reshape(b, s, -1), k.reshape(b, s, -1), v.reshape(b, s, -1), cfg["attn_block"])
            w_out = mla_w_out[j]
        h = _tail(h, y.reshape(t, -1), p[i].reshape(t, -1), w_out, mlp_norm[i], mlp_w1[i], mlp_w2[i],
                  ple_norm[i], ple_gate_w[i], ple_proj_w[i], tm, cfg["ff_chunk"])
    return h.reshape(b, s, d)
```

```python
import functools

import jax
import jax.numpy as jnp
import numpy as np
from jax import lax
from jax.experimental import pallas as pl
from jax.experimental.pallas import tpu as pltpu

EPS = 1e-6
ROPE_THETA = 10000.0
CHUNK = 64
LANES = 128
MXU_WIDTH = 256

RET_HEADS = 4
RET_DK = 256
RET_DV = 512
RET_SUPER = 256

MLA_HEADS = 8
MLA_NOPE = 128
MLA_ROPE = 64
MLA_QKD = MLA_NOPE + MLA_ROPE
MLA_VD = 128
MLA_Q_RANK = 384
MLA_KV_RANK = 256
MLA_HEAD_W = 256

MXU_DTYPE = jnp.bfloat16
F32 = jnp.float32
LOG2_E = 1.4426950408889634
VMEM_LIMIT = 56 * 1024 * 1024

_NT = (((1,), (1,)), ((), ()))
_TN = (((0,), (0,)), ((), ()))


def _dot(a, b):
    return jnp.dot(a, b, preferred_element_type=F32)


def _rms_scale(x):
    return lax.rsqrt(jnp.mean(x * x, axis=-1, keepdims=True) + EPS)


def _row_sumsq(x, weights=None):
    if weights is None:
        weights = jnp.ones((x.shape[-1], LANES), MXU_DTYPE)
    return _dot((x * x).astype(MXU_DTYPE), weights)


def _lanes(a, width):
    return a if width == LANES else jnp.tile(a, (1, width // LANES))


def _sigmoid(x):
    return 0.5 * jnp.tanh(0.5 * x) + 0.5


def _resident(shape):
    zeros = (0,) * len(shape)
    return pl.BlockSpec(shape, lambda *_: zeros, pipeline_mode=pl.Buffered(1))


def _params(*semantics):
    return pltpu.CompilerParams(dimension_semantics=semantics, vmem_limit_bytes=VMEM_LIMIT)


def _rope_angles(seq, dim):
    inv = 1.0 / (ROPE_THETA ** (np.arange(0, dim, 2, dtype=np.float64) / dim))
    ang = np.arange(seq, dtype=np.float64)[:, None] * inv[None, :]
    return np.cos(ang).astype(np.float32), np.sin(ang).astype(np.float32)


def _ret_proj_kernel(x_ref, g_ref, w_ref, cos_ref, sin_ref, gn_ref, q_ref, k_ref, v_ref, gate_ref):
    x = x_ref[...]
    hn = (x * _rms_scale(x) * g_ref[...]).astype(MXU_DTYPE)
    cos = cos_ref[...]
    sin = sin_ref[...]
    half = RET_DK // 2
    qk_w = RET_HEADS * RET_DK
    v_w = RET_HEADS * RET_DV
    k_scale = RET_DK ** -0.5
    for lo in range(0, v_w, MXU_WIDTH):
        g = _dot(hn, w_ref[:, 2 * qk_w + v_w + lo:2 * qk_w + v_w + lo + MXU_WIDTH])
        gate_ref[:, lo:lo + MXU_WIDTH] = (g * _sigmoid(g) * gn_ref[:, lo:lo + MXU_WIDTH]).astype(gate_ref.dtype)
        v_ref[:, lo:lo + MXU_WIDTH] = _dot(
            hn, w_ref[:, 2 * qk_w + lo:2 * qk_w + lo + MXU_WIDTH]).astype(v_ref.dtype)
    for h in range(RET_HEADS):
        lo = h * RET_DK
        r = _dot(hn, w_ref[:, lo:lo + RET_DK])
        x1, x2 = r[:, :half], r[:, half:]
        q_ref[:, lo:lo + half] = (x1 * cos - x2 * sin).astype(q_ref.dtype)
        q_ref[:, lo + half:lo + RET_DK] = (x2 * cos + x1 * sin).astype(q_ref.dtype)
        r = _dot(hn, w_ref[:, qk_w + lo:qk_w + lo + RET_DK])
        x1, x2 = r[:, :half], r[:, half:]
        k_ref[:, lo:lo + half] = ((x1 * cos - x2 * sin) * k_scale).astype(k_ref.dtype)
        k_ref[:, lo + half:lo + RET_DK] = ((x2 * cos + x1 * sin) * k_scale).astype(k_ref.dtype)


def _ret_proj(x2d, gain, w_in, gn_gain, seq, tm):
    t, d = x2d.shape
    qk_w = RET_HEADS * RET_DK
    v_w = RET_HEADS * RET_DV
    cos, sin = _rope_angles(seq, RET_DK)
    pos_blocks = seq // tm
    row = lambda i: (i, 0)
    pos = lambda i: (i % pos_blocks, 0)
    return pl.pallas_call(
        _ret_proj_kernel,
        grid=(t // tm,),
        in_specs=[
            pl.BlockSpec((tm, d), row),
            _resident((1, d)),
            _resident(w_in.shape),
            pl.BlockSpec((tm, RET_DK // 2), pos),
            pl.BlockSpec((tm, RET_DK // 2), pos),
            _resident((1, v_w)),
        ],
        out_specs=[
            pl.BlockSpec((tm, qk_w), row),
            pl.BlockSpec((tm, qk_w), row),
            pl.BlockSpec((tm, v_w), row),
            pl.BlockSpec((tm, v_w), row),
        ],
        out_shape=[
            jax.ShapeDtypeStruct((t, qk_w), MXU_DTYPE),
            jax.ShapeDtypeStruct((t, qk_w), MXU_DTYPE),
            jax.ShapeDtypeStruct((t, v_w), MXU_DTYPE),
            jax.ShapeDtypeStruct((t, v_w), MXU_DTYPE),
        ],
        compiler_params=_params("parallel"),
        name="ret_proj",
    )(x2d, gain.reshape(1, d), w_in.astype(MXU_DTYPE), cos, sin, gn_gain.reshape(1, v_w))


def _retention_kernel(q_ref, k_ref, v_ref, gate_ref, decay_ref, qd_ref, kd_ref, cd_ref, y_ref, state_ref,
                      *, groups):
    @pl.when(pl.program_id(1) == 0)
    def _():
        state_ref[...] = jnp.zeros_like(state_ref)

    def operands(c, h):
        rows = slice(c * RET_SUPER, (c + 1) * RET_SUPER)
        qk_cols = slice(h * RET_DK, (h + 1) * RET_DK)
        return rows, slice(h * RET_DV, (h + 1) * RET_DV), q_ref[0, rows, qk_cols], k_ref[0, rows, qk_cols]

    def in_group(c, h):
        rows, v_cols, q, k = operands(c, h)
        v = v_ref[0, rows, v_cols]
        scores = lax.dot_general(q, k, _NT, preferred_element_type=F32) * decay_ref[h]
        update = lax.dot_general(k * kd_ref[h], v, _TN, preferred_element_type=F32)
        return scores.astype(MXU_DTYPE), update

    def finish(c, h, scores, update):
        rows, v_cols, q, _ = operands(c, h)
        state = state_ref[h]
        lhs = jnp.concatenate([q * qd_ref[h], scores], axis=1)
        rhs = jnp.concatenate([state.astype(MXU_DTYPE), v_ref[0, rows, v_cols]], axis=0)
        out = _dot(lhs, rhs)
        state_ref[h] = state * cd_ref[h] + update
        gate = gate_ref[0, rows, v_cols].astype(F32)
        y_ref[0, rows, v_cols] = (gate * (out * _rms_scale(out))).astype(y_ref.dtype)

    units = [(c, h) for c in range(groups) for h in range(RET_HEADS)]
    ahead = in_group(*units[0])
    for u, unit in enumerate(units):
        current = ahead
        if u + 1 < len(units):
            ahead = in_group(*units[u + 1])
        finish(*unit, *current)


def _retention_tables():
    log_gamma = np.log(1.0 - 2.0 ** (-5.0 - np.arange(RET_HEADS, dtype=np.float64)))
    idx = np.arange(RET_SUPER, dtype=np.float64)
    dist = np.abs(idx[:, None] - idx[None, :])
    chunk = np.arange(RET_SUPER) // CHUNK
    visible = chunk[None, :] <= chunk[:, None]
    decay = np.where(visible[None], np.exp(log_gamma[:, None, None] * dist), 0.0)
    q_decay = np.exp(log_gamma[:, None] * (idx + 1.0))
    k_decay = np.exp(log_gamma[:, None] * (RET_SUPER - 1.0 - idx))
    group_decay = np.exp(log_gamma * RET_SUPER)
    qd = np.broadcast_to(q_decay[:, :, None], (RET_HEADS, RET_SUPER, RET_DK))
    kd = np.broadcast_to(k_decay[:, :, None], (RET_HEADS, RET_SUPER, RET_DK))
    cd = np.broadcast_to(group_decay[:, None, None], (RET_HEADS, 1, RET_DV))
    as_array = lambda a, dtype: jnp.asarray(np.ascontiguousarray(a), dtype)
    return as_array(decay, F32), as_array(qd, MXU_DTYPE), as_array(kd, MXU_DTYPE), as_array(cd, F32)


def _retention(q, k, v, gate, block):
    b, s, _ = q.shape
    qk_w = RET_HEADS * RET_DK
    v_w = RET_HEADS * RET_DV
    decay, qd, kd, cd = _retention_tables()
    tok = lambda i, j: (i, j, 0)
    return pl.pallas_call(
        functools.partial(_retention_kernel, groups=block // RET_SUPER),
        grid=(b, s // block),
        in_specs=[
            pl.BlockSpec((1, block, qk_w), tok),
            pl.BlockSpec((1, block, qk_w), tok),
            pl.BlockSpec((1, block, v_w), tok),
            pl.BlockSpec((1, block, v_w), tok),
            _resident(decay.shape),
            _resident(qd.shape),
            _resident(kd.shape),
            _resident(cd.shape),
        ],
        out_specs=pl.BlockSpec((1, block, v_w), tok),
        out_shape=jax.ShapeDtypeStruct((b, s, v_w), MXU_DTYPE),
        scratch_shapes=[pltpu.VMEM((RET_HEADS, RET_DK, RET_DV), F32)],
        compiler_params=_params("parallel", "arbitrary"),
        name="retention",
    )(q, k, v, gate, decay, qd, kd, cd)


def _tail_kernel(h_ref, y_ref, p_ref, wo_ref, mg_ref, w1_ref, w2_ref, pg_ref, wg_ref, wp_ref, o_ref,
                 *, ff_chunk):
    h = h_ref[...] + _dot(y_ref[...], wo_ref[...])
    hn = (h * _rms_scale(h) * mg_ref[...]).astype(MXU_DTYPE)
    d_ff = w1_ref.shape[1]
    for c in range(d_ff // ff_chunk):
        cols = slice(c * ff_chunk, (c + 1) * ff_chunk)
        a = jnp.square(jnp.maximum(_dot(hn, w1_ref[:, cols]), 0.0))
        h = h + _dot(a.astype(MXU_DTYPE), w2_ref[cols, :])
    hn = (h * _rms_scale(h) * pg_ref[...]).astype(MXU_DTYPE)
    gate = _sigmoid(_dot(hn, wg_ref[...]))
    emb = _dot(p_ref[...].astype(MXU_DTYPE), wp_ref[...])
    o_ref[...] = h + gate * emb


def _layer_of(shape, layer):
    index = (layer,) + (0,) * (len(shape) - 1)
    return pl.BlockSpec((None,) + tuple(shape[1:]), lambda *_: index, pipeline_mode=pl.Buffered(1))


def _tail(h2d, y2d, p3d, w_out, mlp_gain, w1, w2, ple_gain, wg, wp, layer, tm, ff_chunk):
    t, d = h2d.shape
    row = lambda i: (i, 0)
    return pl.pallas_call(
        functools.partial(_tail_kernel, ff_chunk=ff_chunk),
        grid=(t // tm,),
        in_specs=[
            pl.BlockSpec((tm, d), row),
            pl.BlockSpec((tm, y2d.shape[1]), row),
            pl.BlockSpec((None, tm, p3d.shape[2]), lambda i: (layer, i, 0)),
            _resident(w_out.shape),
            _layer_of(mlp_gain.shape, layer),
            _layer_of(w1.shape, layer),
            _layer_of(w2.shape, layer),
            _layer_of(ple_gain.shape, layer),
            _layer_of(wg.shape, layer),
            _layer_of(wp.shape, layer),
        ],
        out_specs=pl.BlockSpec((tm, d), row),
        out_shape=jax.ShapeDtypeStruct((t, d), F32),
        compiler_params=_params("parallel"),
        name="layer_tail",
    )(h2d, y2d, p3d, w_out.astype(MXU_DTYPE), mlp_gain, w1, w2, ple_gain, wg, wp)


def _mla_proj_kernel(x_ref, g_ref, win_ref, qa_ref, kva_ref, wuq_ref, wukv_ref, qn_ref, kn_ref,
                     qrope_ref, krope_ref, q_ref, k_ref, v_ref, *, sub_tiles):
    tm = x_ref.shape[0] // sub_tiles
    d = x_ref.shape[-1]
    lane = lax.broadcasted_iota(jnp.int32, (1, LANES), 1)
    first_half = (lane < MLA_ROPE).astype(F32)
    softmax_scale = MLA_QKD ** -0.5 * LOG2_E
    qn = qn_ref[...]
    kn = kn_ref[...]
    ones = jnp.ones((MLA_NOPE, LANES), MXU_DTYPE)
    halves = jnp.full((LANES, LANES), 0.5, MXU_DTYPE)
    head_weights = jnp.concatenate([ones, halves], axis=0)

    def down(rows):
        x = x_ref[rows, :]
        inv = lax.rsqrt(_row_sumsq(x) * (1.0 / d) + EPS)
        hn = (x * _lanes(inv, d) * g_ref[...]).astype(MXU_DTYPE)
        proj = _dot(hn, win_ref[...])
        c_q = proj[:, :MLA_Q_RANK]
        c_kv = proj[:, MLA_Q_RANK:MLA_Q_RANK + MLA_KV_RANK]
        k_r = proj[:, MLA_Q_RANK + MLA_KV_RANK:]
        inv = lax.rsqrt(_row_sumsq(c_q) * (1.0 / MLA_Q_RANK) + EPS)
        c_q = (c_q * _lanes(inv, MLA_Q_RANK) * qa_ref[...]).astype(MXU_DTYPE)
        inv = lax.rsqrt(_row_sumsq(c_kv) * (1.0 / MLA_KV_RANK) + EPS)
        c_kv = (c_kv * _lanes(inv, MLA_KV_RANK) * kva_ref[...]).astype(MXU_DTYPE)
        k_rot = k_r * krope_ref[rows, :]
        k_rot = (k_rot + pltpu.roll(k_rot, MLA_ROPE, 1)) * first_half
        return c_q, c_kv, k_rot, _row_sumsq(k_r, halves)

    def heads(rows, c_q, c_kv, k_rot, k_r_ss):
        def up_project(h):
            cols = slice(h * MLA_HEAD_W, (h + 1) * MLA_HEAD_W)
            return _dot(c_q, wuq_ref[:, cols]), _dot(c_kv, wukv_ref[:, cols])

        ahead = up_project(0)
        for h in range(MLA_HEADS):
            lo = h * MLA_HEAD_W
            qh, kvh = ahead
            if h + 1 < MLA_HEADS:
                ahead = up_project(h + 1)
            q_nope, q_r = qh[:, :MLA_NOPE], qh[:, MLA_NOPE:]
            inv = lax.rsqrt(_row_sumsq(qh, head_weights) * (1.0 / MLA_QKD) + EPS) * softmax_scale
            q_rot = q_r * qrope_ref[rows, :]
            q_rot = q_rot + pltpu.roll(q_rot, MLA_ROPE, 1)
            q_ref[rows, lo:lo + MLA_NOPE] = (q_nope * inv * qn).astype(q_ref.dtype)
            q_ref[rows, lo + MLA_NOPE:lo + MLA_HEAD_W] = (q_rot * inv).astype(q_ref.dtype)

            k_nope, v = kvh[:, :MLA_NOPE], kvh[:, MLA_NOPE:]
            inv = lax.rsqrt((_row_sumsq(k_nope, ones) + k_r_ss) * (1.0 / MLA_QKD) + EPS)
            k_ref[rows, lo:lo + MLA_NOPE] = (k_nope * inv * kn).astype(k_ref.dtype)
            k_ref[rows, lo + MLA_NOPE:lo + MLA_HEAD_W] = (k_rot * inv).astype(k_ref.dtype)
            v_ref[rows, h * MLA_VD:(h + 1) * MLA_VD] = v.astype(v_ref.dtype)

    tiles = [slice(i * tm, (i + 1) * tm) for i in range(sub_tiles)]
    projected = [down(rows) for rows in tiles]
    for rows, operands in zip(tiles, projected):
        heads(rows, *operands)


def _rotate_half_cols(w):
    half = w.shape[-1] // 2
    return jnp.concatenate([-w[..., half:], w[..., :half]], axis=-1)


def _rope_gain_table(seq, gain_rope):
    cos, sin = _rope_angles(seq, MLA_ROPE)
    base = jnp.asarray(np.concatenate([cos, cos, sin, sin], axis=-1))
    half = MLA_ROPE // 2
    gains = jnp.concatenate([gain_rope, gain_rope[half:], gain_rope[:half]])
    return base * gains[None, :]


def _mla_proj(x2d, gain, w_in, q_a_gain, kv_a_gain, w_uq, w_ukv, q_gain, k_gain, seq, tm, sub_tiles):
    t, d = x2d.shape
    kr_lo = MLA_Q_RANK + MLA_KV_RANK
    w_in_ext = jnp.concatenate([w_in, _rotate_half_cols(w_in[:, kr_lo:])], axis=1)
    w_uq_h = w_uq.reshape(MLA_Q_RANK, MLA_HEADS, MLA_QKD)
    w_uq_ext = jnp.concatenate([w_uq_h, _rotate_half_cols(w_uq_h[..., MLA_NOPE:])], axis=-1)
    w_uq_ext = w_uq_ext.reshape(MLA_Q_RANK, MLA_HEADS * MLA_HEAD_W)
    q_rope = _rope_gain_table(seq, q_gain[MLA_NOPE:])
    k_rope = _rope_gain_table(seq, k_gain[MLA_NOPE:])
    qk_w = MLA_HEADS * MLA_HEAD_W
    v_w = MLA_HEADS * MLA_VD
    block = sub_tiles * tm
    pos_blocks = seq // block
    row = lambda i: (i, 0)
    pos = lambda i: (i % pos_blocks, 0)
    return pl.pallas_call(
        functools.partial(_mla_proj_kernel, sub_tiles=sub_tiles),
        grid=(t // block,),
        in_specs=[
            pl.BlockSpec((block, d), row),
            _resident((1, d)),
            _resident(w_in_ext.shape),
            _resident((1, MLA_Q_RANK)),
            _resident((1, MLA_KV_RANK)),
            _resident(w_uq_ext.shape),
            _resident(w_ukv.shape),
            _resident((1, MLA_NOPE)),
            _resident((1, MLA_NOPE)),
            pl.BlockSpec((block, LANES), pos),
            pl.BlockSpec((block, LANES), pos),
        ],
        out_specs=[
            pl.BlockSpec((block, qk_w), row),
            pl.BlockSpec((block, qk_w), row),
            pl.BlockSpec((block, v_w), row),
        ],
        out_shape=[
            jax.ShapeDtypeStruct((t, qk_w), MXU_DTYPE),
            jax.ShapeDtypeStruct((t, qk_w), MXU_DTYPE),
            jax.ShapeDtypeStruct((t, v_w), MXU_DTYPE),
        ],
        compiler_params=_params("parallel"),
        name="mla_proj",
    )(x2d, gain.reshape(1, d), w_in_ext.astype(MXU_DTYPE), q_a_gain.reshape(1, -1), kv_a_gain.reshape(1, -1),
      w_uq_ext.astype(MXU_DTYPE), w_ukv.astype(MXU_DTYPE), q_gain[:MLA_NOPE].reshape(1, -1),
      k_gain[:MLA_NOPE].reshape(1, -1), q_rope, k_rope)


def _attn_kernel(q_ref, k_ref, v_ref, o_ref, m_ref, acc_ref, *, block):
    n_blocks = q_ref.shape[1] // block
    ones = jnp.ones((block, LANES), MXU_DTYPE)
    pairs = [(qb, kb) for qb in range(n_blocks) for kb in range(qb + 1)]

    def rows(i):
        return slice(i * block, (i + 1) * block)

    def scores(qb, kb):
        return lax.dot_general(q_ref[0, rows(qb), :], k_ref[0, rows(kb), :], _NT, preferred_element_type=F32)

    def update(s, qb, kb):
        slot = qb % 2
        if kb == qb:
            q_chunk = lax.broadcasted_iota(jnp.int32, s.shape, 0) // CHUNK
            k_chunk = lax.broadcasted_iota(jnp.int32, s.shape, 1) // CHUNK
            s = jnp.where(k_chunk <= q_chunk, s, -1e30)
        m_new = jnp.broadcast_to(jnp.max(s, axis=-1, keepdims=True), (block, LANES))
        if kb > 0:
            m_prev = m_ref[slot]
            m_new = jnp.maximum(m_prev, m_new)
        p = jnp.exp2(s - jnp.tile(m_new, (1, block // LANES)))
        acc = _dot(p.astype(MXU_DTYPE), jnp.concatenate([v_ref[0, rows(kb), :], ones], axis=1))
        if kb > 0:
            acc = jnp.tile(jnp.exp2(m_prev - m_new), (1, 2)) * acc_ref[slot] + acc
        if kb == qb:
            o_ref[0, rows(qb), :] = (acc[:, :MLA_VD] / acc[:, MLA_VD:]).astype(o_ref.dtype)
        else:
            m_ref[slot] = m_new
            acc_ref[slot] = acc

    s_next = scores(*pairs[0])
    for u, (qb, kb) in enumerate(pairs):
        s = s_next
        if u + 1 < len(pairs):
            s_next = scores(*pairs[u + 1])
        update(s, qb, kb)


def _attention(q, k, v, block):
    b, s, _ = q.shape
    return pl.pallas_call(
        functools.partial(_attn_kernel, block=block),
        grid=(b, MLA_HEADS),
        in_specs=[
            pl.BlockSpec((1, s, MLA_HEAD_W), lambda i, h: (i, 0, h)),
            pl.BlockSpec((1, s, MLA_HEAD_W), lambda i, h: (i, 0, h)),
            pl.BlockSpec((1, s, MLA_VD), lambda i, h: (i, 0, h)),
        ],
        out_specs=pl.BlockSpec((1, s, MLA_VD), lambda i, h: (i, 0, h)),
        out_shape=jax.ShapeDtypeStruct((b, s, MLA_HEADS * MLA_VD), MXU_DTYPE),
        scratch_shapes=[
            pltpu.VMEM((2, block, LANES), F32),
            pltpu.VMEM((2, block, MLA_VD + LANES), F32),
        ],
        compiler_params=_params("parallel", "parallel"),
        name="mla_attention",
    )(q, k, v)


def _tiles(seq):
    assert seq % RET_SUPER == 0, seq
    tm = 512 if seq % 512 == 0 else RET_SUPER
    mla_sub_tiles = 2 if seq % (2 * tm) == 0 else 1
    return dict(tm=tm, ret_block=tm, attn_block=tm, ff_chunk=1024, mla_sub_tiles=mla_sub_tiles)


def kernel(x, p, mix_norm, ret_w_in, ret_gn, ret_w_out, mla_w_in, mla_q_a_norm, mla_kv_a_norm, mla_w_uq,
           mla_w_ukv, mla_q_norm, mla_k_norm, mla_w_out, mlp_norm, mlp_w1, mlp_w2, ple_norm, ple_gate_w,
           ple_proj_w):
    b, s, d = x.shape
    depth = p.shape[0]
    t = b * s
    cfg = _tiles(s)
    tm = cfg["tm"]
    h = x.reshape(t, d)
    p3d = p.reshape(depth, t, -1)
    mlp_gain = mlp_norm.reshape(depth, 1, d)
    ple_gain = ple_norm.reshape(depth, 1, d)
    w1, w2, wg, wp = (w.astype(MXU_DTYPE) for w in (mlp_w1, mlp_w2, ple_gate_w, ple_proj_w))
    for i in range(depth):
        j = i // 2
        if i % 2 == 0:
            q, k, v, gate = _ret_proj(h, mix_norm[i], ret_w_in[j], ret_gn[j], s, tm)
            y = _retention(q.reshape(b, s, -1), k.reshape(b, s, -1), v.reshape(b, s, -1),
                           gate.reshape(b, s, -1), cfg["ret_block"])
            w_out = ret_w_out[j]
        else:
            q, k, v = _mla_proj(h, mix_norm[i], mla_w_in[j], mla_q_a_norm[j], mla_kv_a_norm[j], mla_w_uq[j],
                                mla_w_ukv[j], mla_q_norm[j], mla_k_norm[j], s, tm, cfg["mla_sub_tiles"])
            y = _attention(q.reshape(b, s, -1), k.reshape(b, s, -1), v.reshape(b, s, -1), cfg["attn_block"])
            w_out = mla_w_out[j]
        h = _tail(h, y.reshape(t, -1), p3d, w_out, mlp_gain, w1, w2, ple_gain, wg, wp, i, tm, cfg["ff_chunk"])
    return h.reshape(b, s, d)
```

```python
import functools

import jax
import jax.numpy as jnp
import numpy as np
from jax import lax
from jax.experimental import pallas as pl
from jax.experimental.pallas import tpu as pltpu

EPS = 1e-6
ROPE_THETA = 10000.0
CHUNK = 64
LANES = 128
MXU_WIDTH = 256

RET_HEADS = 4
RET_DK = 256
RET_DV = 512
RET_SUPER = 256

MLA_HEADS = 8
MLA_NOPE = 128
MLA_ROPE = 64
MLA_QKD = MLA_NOPE + MLA_ROPE
MLA_VD = 128
MLA_Q_RANK = 384
MLA_KV_RANK = 256
MLA_HEAD_W = 256

MXU_DTYPE = jnp.bfloat16
F32 = jnp.float32
LOG2_E = 1.4426950408889634
VMEM_LIMIT = 56 * 1024 * 1024

_NT = (((1,), (1,)), ((), ()))
_TN = (((0,), (0,)), ((), ()))


def _dot(a, b):
    return jnp.dot(a, b, preferred_element_type=F32)


def _rms_scale(x):
    return lax.rsqrt(jnp.mean(x * x, axis=-1, keepdims=True) + EPS)


def _row_sumsq(x, weights=None):
    if weights is None:
        weights = jnp.ones((x.shape[-1], LANES), MXU_DTYPE)
    return _dot((x * x).astype(MXU_DTYPE), weights)


def _lanes(a, width):
    return a if width == LANES else jnp.tile(a, (1, width // LANES))


def _sigmoid(x):
    return 0.5 * jnp.tanh(0.5 * x) + 0.5


def _resident(shape):
    zeros = (0,) * len(shape)
    return pl.BlockSpec(shape, lambda *_: zeros, pipeline_mode=pl.Buffered(1))


def _params(*semantics):
    return pltpu.CompilerParams(dimension_semantics=semantics, vmem_limit_bytes=VMEM_LIMIT)


def _rope_angles(seq, dim):
    inv = 1.0 / (ROPE_THETA ** (jnp.arange(0, dim, 2, dtype=F32) / dim))
    ang = jnp.arange(seq, dtype=F32)[:, None] * inv[None, :]
    return jnp.cos(ang), jnp.sin(ang)


def _ret_proj_kernel(x_ref, g_ref, w_ref, cos_ref, sin_ref, gn_ref, q_ref, k_ref, v_ref, gate_ref):
    x = x_ref[...]
    hn = (x * _rms_scale(x) * g_ref[...]).astype(MXU_DTYPE)
    cos = cos_ref[...]
    sin = sin_ref[...]
    half = RET_DK // 2
    qk_w = RET_HEADS * RET_DK
    v_w = RET_HEADS * RET_DV
    k_scale = RET_DK ** -0.5
    for lo in range(0, v_w, MXU_WIDTH):
        g = _dot(hn, w_ref[:, 2 * qk_w + v_w + lo:2 * qk_w + v_w + lo + MXU_WIDTH])
        gate_ref[:, lo:lo + MXU_WIDTH] = (g * _sigmoid(g) * gn_ref[:, lo:lo + MXU_WIDTH]).astype(gate_ref.dtype)
        v_ref[:, lo:lo + MXU_WIDTH] = _dot(
            hn, w_ref[:, 2 * qk_w + lo:2 * qk_w + lo + MXU_WIDTH]).astype(v_ref.dtype)
    for h in range(RET_HEADS):
        lo = h * RET_DK
        r = _dot(hn, w_ref[:, lo:lo + RET_DK])
        x1, x2 = r[:, :half], r[:, half:]
        q_ref[:, lo:lo + half] = (x1 * cos - x2 * sin).astype(q_ref.dtype)
        q_ref[:, lo + half:lo + RET_DK] = (x2 * cos + x1 * sin).astype(q_ref.dtype)
        r = _dot(hn, w_ref[:, qk_w + lo:qk_w + lo + RET_DK])
        x1, x2 = r[:, :half], r[:, half:]
        k_ref[:, lo:lo + half] = ((x1 * cos - x2 * sin) * k_scale).astype(k_ref.dtype)
        k_ref[:, lo + half:lo + RET_DK] = ((x2 * cos + x1 * sin) * k_scale).astype(k_ref.dtype)


def _ret_proj(x2d, gain, w_in, gn_gain, seq, tm):
    t, d = x2d.shape
    qk_w = RET_HEADS * RET_DK
    v_w = RET_HEADS * RET_DV
    cos, sin = _rope_angles(seq, RET_DK)
    pos_blocks = seq // tm
    row = lambda i: (i, 0)
    pos = lambda i: (i % pos_blocks, 0)
    return pl.pallas_call(
        _ret_proj_kernel,
        grid=(t // tm,),
        in_specs=[
            pl.BlockSpec((tm, d), row),
            _resident((1, d)),
            _resident(w_in.shape),
            pl.BlockSpec((tm, RET_DK // 2), pos),
            pl.BlockSpec((tm, RET_DK // 2), pos),
            _resident((1, v_w)),
        ],
        out_specs=[
            pl.BlockSpec((tm, qk_w), row),
            pl.BlockSpec((tm, qk_w), row),
            pl.BlockSpec((tm, v_w), row),
            pl.BlockSpec((tm, v_w), row),
        ],
        out_shape=[
            jax.ShapeDtypeStruct((t, qk_w), MXU_DTYPE),
            jax.ShapeDtypeStruct((t, qk_w), MXU_DTYPE),
            jax.ShapeDtypeStruct((t, v_w), MXU_DTYPE),
            jax.ShapeDtypeStruct((t, v_w), MXU_DTYPE),
        ],
        compiler_params=_params("parallel"),
        name="ret_proj",
    )(x2d, gain.reshape(1, d), w_in.astype(MXU_DTYPE), cos, sin, gn_gain.reshape(1, v_w))


def _retention_kernel(q_ref, k_ref, v_ref, gate_ref, decay_ref, qd_ref, kd_ref, cd_ref, y_ref, state_ref,
                      *, groups):
    @pl.when(pl.program_id(1) == 0)
    def _():
        state_ref[...] = jnp.zeros_like(state_ref)

    def matmuls(c, h):
        rows = slice(c * RET_SUPER, (c + 1) * RET_SUPER)
        qk_cols = slice(h * RET_DK, (h + 1) * RET_DK)
        v_cols = slice(h * RET_DV, (h + 1) * RET_DV)
        q = q_ref[0, rows, qk_cols]
        k = k_ref[0, rows, qk_cols]
        v = v_ref[0, rows, v_cols]
        state = state_ref[h]
        scores = lax.dot_general(q, k, _NT, preferred_element_type=F32) * decay_ref[h]
        lhs = jnp.concatenate([q * qd_ref[h], scores.astype(MXU_DTYPE)], axis=1)
        rhs = jnp.concatenate([state.astype(MXU_DTYPE), v], axis=0)
        out = _dot(lhs, rhs)
        state_ref[h] = state * cd_ref[h] + lax.dot_general(k * kd_ref[h], v, _TN, preferred_element_type=F32)
        return out

    def finish(c, h, out):
        rows = slice(c * RET_SUPER, (c + 1) * RET_SUPER)
        v_cols = slice(h * RET_DV, (h + 1) * RET_DV)
        gate = gate_ref[0, rows, v_cols].astype(F32)
        y_ref[0, rows, v_cols] = (gate * (out * _rms_scale(out))).astype(y_ref.dtype)

    units = [(c, h) for c in range(groups) for h in range(RET_HEADS)]
    ahead = matmuls(*units[0])
    for u, unit in enumerate(units):
        out = ahead
        if u + 1 < len(units):
            ahead = matmuls(*units[u + 1])
        finish(*unit, out)


def _retention_tables():
    log_gamma = np.log(1.0 - 2.0 ** (-5.0 - np.arange(RET_HEADS, dtype=np.float64)))
    idx = np.arange(RET_SUPER, dtype=np.float64)
    dist = np.abs(idx[:, None] - idx[None, :])
    chunk = np.arange(RET_SUPER) // CHUNK
    visible = chunk[None, :] <= chunk[:, None]
    decay = np.where(visible[None], np.exp(log_gamma[:, None, None] * dist), 0.0)
    q_decay = np.exp(log_gamma[:, None] * (idx + 1.0))
    k_decay = np.exp(log_gamma[:, None] * (RET_SUPER - 1.0 - idx))
    group_decay = np.exp(log_gamma * RET_SUPER)
    qd = np.broadcast_to(q_decay[:, :, None], (RET_HEADS, RET_SUPER, RET_DK))
    kd = np.broadcast_to(k_decay[:, :, None], (RET_HEADS, RET_SUPER, RET_DK))
    cd = np.broadcast_to(group_decay[:, None, None], (RET_HEADS, 1, RET_DV))
    as_array = lambda a, dtype: jnp.asarray(np.ascontiguousarray(a), dtype)
    return as_array(decay, F32), as_array(qd, MXU_DTYPE), as_array(kd, MXU_DTYPE), as_array(cd, F32)


def _retention(q, k, v, gate, block):
    b, s, _ = q.shape
    qk_w = RET_HEADS * RET_DK
    v_w = RET_HEADS * RET_DV
    decay, qd, kd, cd = _retention_tables()
    tok = lambda i, j: (i, j, 0)
    return pl.pallas_call(
        functools.partial(_retention_kernel, groups=block // RET_SUPER),
        grid=(b, s // block),
        in_specs=[
            pl.BlockSpec((1, block, qk_w), tok),
            pl.BlockSpec((1, block, qk_w), tok),
            pl.BlockSpec((1, block, v_w), tok),
            pl.BlockSpec((1, block, v_w), tok),
            _resident(decay.shape),
            _resident(qd.shape),
            _resident(kd.shape),
            _resident(cd.shape),
        ],
        out_specs=pl.BlockSpec((1, block, v_w), tok),
        out_shape=jax.ShapeDtypeStruct((b, s, v_w), MXU_DTYPE),
        scratch_shapes=[pltpu.VMEM((RET_HEADS, RET_DK, RET_DV), F32)],
        compiler_params=_params("parallel", "arbitrary"),
        name="retention",
    )(q, k, v, gate, decay, qd, kd, cd)


def _tail_kernel(h_ref, y_ref, p_ref, wo_ref, mg_ref, w1_ref, w2_ref, pg_ref, wg_ref, wp_ref, o_ref,
                 *, ff_chunk):
    h = h_ref[...] + _dot(y_ref[...], wo_ref[...])
    hn = (h * _rms_scale(h) * mg_ref[...]).astype(MXU_DTYPE)
    d_ff = w1_ref.shape[1]
    for c in range(d_ff // ff_chunk):
        cols = slice(c * ff_chunk, (c + 1) * ff_chunk)
        a = jnp.square(jnp.maximum(_dot(hn, w1_ref[:, cols]), 0.0))
        h = h + _dot(a.astype(MXU_DTYPE), w2_ref[cols, :])
    hn = (h * _rms_scale(h) * pg_ref[...]).astype(MXU_DTYPE)
    gate = _sigmoid(_dot(hn, wg_ref[...]))
    emb = _dot(p_ref[...].astype(MXU_DTYPE), wp_ref[...])
    o_ref[...] = h + gate * emb


def _layer_of(shape, layer):
    index = (layer,) + (0,) * (len(shape) - 1)
    return pl.BlockSpec((None,) + tuple(shape[1:]), lambda *_: index, pipeline_mode=pl.Buffered(1))


def _tail(h2d, y2d, p3d, w_out, mlp_gain, w1, w2, ple_gain, wg, wp, layer, tm, ff_chunk):
    t, d = h2d.shape
    row = lambda i: (i, 0)
    return pl.pallas_call(
        functools.partial(_tail_kernel, ff_chunk=ff_chunk),
        grid=(t // tm,),
        in_specs=[
            pl.BlockSpec((tm, d), row),
            pl.BlockSpec((tm, y2d.shape[1]), row),
            pl.BlockSpec((None, tm, p3d.shape[2]), lambda i: (layer, i, 0)),
            _resident(w_out.shape),
            _layer_of(mlp_gain.shape, layer),
            _layer_of(w1.shape, layer),
            _layer_of(w2.shape, layer),
            _layer_of(ple_gain.shape, layer),
            _layer_of(wg.shape, layer),
            _layer_of(wp.shape, layer),
        ],
        out_specs=pl.BlockSpec((tm, d), row),
        out_shape=jax.ShapeDtypeStruct((t, d), F32),
        compiler_params=_params("parallel"),
        name="layer_tail",
    )(h2d, y2d, p3d, w_out.astype(MXU_DTYPE), mlp_gain, w1, w2, ple_gain, wg, wp)


def _mla_proj_kernel(x_ref, g_ref, win_ref, qa_ref, kva_ref, wuq_ref, wukv_ref, qn_ref, kn_ref,
                     qrope_ref, krope_ref, q_ref, k_ref, v_ref, *, sub_tiles):
    tm = x_ref.shape[0] // sub_tiles
    d = x_ref.shape[-1]
    lane = lax.broadcasted_iota(jnp.int32, (1, LANES), 1)
    first_half = (lane < MLA_ROPE).astype(F32)
    softmax_scale = MLA_QKD ** -0.5 * LOG2_E
    qn = qn_ref[...]
    kn = kn_ref[...]
    ones = jnp.ones((MLA_NOPE, LANES), MXU_DTYPE)
    halves = jnp.full((LANES, LANES), 0.5, MXU_DTYPE)
    head_weights = jnp.concatenate([ones, halves], axis=0)

    def down(rows):
        x = x_ref[rows, :]
        inv = lax.rsqrt(_row_sumsq(x) * (1.0 / d) + EPS)
        hn = (x * _lanes(inv, d) * g_ref[...]).astype(MXU_DTYPE)
        proj = _dot(hn, win_ref[...])
        c_q = proj[:, :MLA_Q_RANK]
        c_kv = proj[:, MLA_Q_RANK:MLA_Q_RANK + MLA_KV_RANK]
        k_r = proj[:, MLA_Q_RANK + MLA_KV_RANK:]
        inv = lax.rsqrt(_row_sumsq(c_q) * (1.0 / MLA_Q_RANK) + EPS)
        c_q = (c_q * _lanes(inv, MLA_Q_RANK) * qa_ref[...]).astype(MXU_DTYPE)
        inv = lax.rsqrt(_row_sumsq(c_kv) * (1.0 / MLA_KV_RANK) + EPS)
        c_kv = (c_kv * _lanes(inv, MLA_KV_RANK) * kva_ref[...]).astype(MXU_DTYPE)
        k_rot = k_r * krope_ref[rows, :]
        k_rot = (k_rot + pltpu.roll(k_rot, MLA_ROPE, 1)) * first_half
        return c_q, c_kv, k_rot, _row_sumsq(k_r, halves)

    def heads(rows, c_q, c_kv, k_rot, k_r_ss):
        def up_project(h):
            cols = slice(h * MLA_HEAD_W, (h + 1) * MLA_HEAD_W)
            return _dot(c_q, wuq_ref[:, cols]), _dot(c_kv, wukv_ref[:, cols])

        ahead = up_project(0)
        for h in range(MLA_HEADS):
            lo = h * MLA_HEAD_W
            qh, kvh = ahead
            if h + 1 < MLA_HEADS:
                ahead = up_project(h + 1)
            q_nope, q_r = qh[:, :MLA_NOPE], qh[:, MLA_NOPE:]
            inv = lax.rsqrt(_row_sumsq(qh, head_weights) * (1.0 / MLA_QKD) + EPS) * softmax_scale
            q_rot = q_r * qrope_ref[rows, :]
            q_rot = q_rot + pltpu.roll(q_rot, MLA_ROPE, 1)
            q_ref[rows, lo:lo + MLA_NOPE] = (q_nope * inv * qn).astype(q_ref.dtype)
            q_ref[rows, lo + MLA_NOPE:lo + MLA_HEAD_W] = (q_rot * inv).astype(q_ref.dtype)

            k_nope, v = kvh[:, :MLA_NOPE], kvh[:, MLA_NOPE:]
            inv = lax.rsqrt((_row_sumsq(k_nope, ones) + k_r_ss) * (1.0 / MLA_QKD) + EPS)
            k_ref[rows, lo:lo + MLA_NOPE] = (k_nope * inv * kn).astype(k_ref.dtype)
            k_ref[rows, lo + MLA_NOPE:lo + MLA_HEAD_W] = (k_rot * inv).astype(k_ref.dtype)
            v_ref[rows, h * MLA_VD:(h + 1) * MLA_VD] = v.astype(v_ref.dtype)

    tiles = [slice(i * tm, (i + 1) * tm) for i in range(sub_tiles)]
    projected = [down(rows) for rows in tiles]
    for rows, operands in zip(tiles, projected):
        heads(rows, *operands)


def _rotate_half_cols(w):
    half = w.shape[-1] // 2
    return jnp.concatenate([-w[..., half:], w[..., :half]], axis=-1)


def _rope_gain_table(seq, gain_rope):
    cos, sin = _rope_angles(seq, MLA_ROPE)
    base = jnp.concatenate([cos, cos, sin, sin], axis=-1)
    half = MLA_ROPE // 2
    gains = jnp.concatenate([gain_rope, gain_rope[half:], gain_rope[:half]])
    return base * gains[None, :]


def _mla_proj(x2d, gain, w_in, q_a_gain, kv_a_gain, w_uq, w_ukv, q_gain, k_gain, seq, tm, sub_tiles):
    t, d = x2d.shape
    kr_lo = MLA_Q_RANK + MLA_KV_RANK
    w_in_ext = jnp.concatenate([w_in, _rotate_half_cols(w_in[:, kr_lo:])], axis=1)
    w_uq_h = w_uq.reshape(MLA_Q_RANK, MLA_HEADS, MLA_QKD)
    w_uq_ext = jnp.concatenate([w_uq_h, _rotate_half_cols(w_uq_h[..., MLA_NOPE:])], axis=-1)
    w_uq_ext = w_uq_ext.reshape(MLA_Q_RANK, MLA_HEADS * MLA_HEAD_W)
    q_rope = _rope_gain_table(seq, q_gain[MLA_NOPE:])
    k_rope = _rope_gain_table(seq, k_gain[MLA_NOPE:])
    qk_w = MLA_HEADS * MLA_HEAD_W
    v_w = MLA_HEADS * MLA_VD
    block = sub_tiles * tm
    pos_blocks = seq // block
    row = lambda i: (i, 0)
    pos = lambda i: (i % pos_blocks, 0)
    return pl.pallas_call(
        functools.partial(_mla_proj_kernel, sub_tiles=sub_tiles),
        grid=(t // block,),
        in_specs=[
            pl.BlockSpec((block, d), row),
            _resident((1, d)),
            _resident(w_in_ext.shape),
            _resident((1, MLA_Q_RANK)),
            _resident((1, MLA_KV_RANK)),
            _resident(w_uq_ext.shape),
            _resident(w_ukv.shape),
            _resident((1, MLA_NOPE)),
            _resident((1, MLA_NOPE)),
            pl.BlockSpec((block, LANES), pos),
            pl.BlockSpec((block, LANES), pos),
        ],
        out_specs=[
            pl.BlockSpec((block, qk_w), row),
            pl.BlockSpec((block, qk_w), row),
            pl.BlockSpec((block, v_w), row),
        ],
        out_shape=[
            jax.ShapeDtypeStruct((t, qk_w), MXU_DTYPE),
            jax.ShapeDtypeStruct((t, qk_w), MXU_DTYPE),
            jax.ShapeDtypeStruct((t, v_w), MXU_DTYPE),
        ],
        compiler_params=_params("parallel"),
        name="mla_proj",
    )(x2d, gain.reshape(1, d), w_in_ext.astype(MXU_DTYPE), q_a_gain.reshape(1, -1), kv_a_gain.reshape(1, -1),
      w_uq_ext.astype(MXU_DTYPE), w_ukv.astype(MXU_DTYPE), q_gain[:MLA_NOPE].reshape(1, -1),
      k_gain[:MLA_NOPE].reshape(1, -1), q_rope, k_rope)


def _attn_kernel(q_ref, k_ref, v_ref, o_ref, m_ref, acc_ref, *, block):
    n_blocks = q_ref.shape[1] // block
    half = block // 2

    units = []
    for qb in range(n_blocks):
        q0 = qb * block
        units += [((q0, block), (kb * block, block)) for kb in range(qb)]
        units += [((q0, half), (q0, half)), ((q0 + half, half), (q0, block))]

    def span(start_size):
        return slice(start_size[0], start_size[0] + start_size[1])

    def scores(q_rows, k_rows):
        return lax.dot_general(q_ref[0, span(q_rows), :], k_ref[0, span(k_rows), :], _NT,
                               preferred_element_type=F32)

    def update(s, q_rows, k_rows):
        (q0, nq), (k0, nk) = q_rows, k_rows
        slot = (q0 // block) % 2
        local = slice(q0 % block, q0 % block + nq)
        first = k0 == 0
        last = k0 + nk == q0 + nq
        if last:
            q_chunk = (q0 + lax.broadcasted_iota(jnp.int32, s.shape, 0)) // CHUNK
            k_chunk = (k0 + lax.broadcasted_iota(jnp.int32, s.shape, 1)) // CHUNK
            s = jnp.where(k_chunk <= q_chunk, s, -1e30)
        m_new = jnp.broadcast_to(jnp.max(s, axis=-1, keepdims=True), (nq, LANES))
        if not first:
            m_prev = m_ref[slot, local, :]
            m_new = jnp.maximum(m_prev, m_new)
        p = jnp.exp2(s - jnp.tile(m_new, (1, nk // LANES)))
        ones = jnp.ones((nk, LANES), MXU_DTYPE)
        acc = _dot(p.astype(MXU_DTYPE), jnp.concatenate([v_ref[0, span(k_rows), :], ones], axis=1))
        if not first:
            acc = jnp.tile(jnp.exp2(m_prev - m_new), (1, 2)) * acc_ref[slot, local, :] + acc
        if last:
            o_ref[0, span(q_rows), :] = (acc[:, :MLA_VD] / acc[:, MLA_VD:]).astype(o_ref.dtype)
        else:
            m_ref[slot, local, :] = m_new
            acc_ref[slot, local, :] = acc

    s_next = scores(*units[0])
    for u, unit in enumerate(units):
        s = s_next
        if u + 1 < len(units):
            s_next = scores(*units[u + 1])
        update(s, *unit)


def _attention(q, k, v, block):
    b, s, _ = q.shape
    return pl.pallas_call(
        functools.partial(_attn_kernel, block=block),
        grid=(b, MLA_HEADS),
        in_specs=[
            pl.BlockSpec((1, s, MLA_HEAD_W), lambda i, h: (i, 0, h)),
            pl.BlockSpec((1, s, MLA_HEAD_W), lambda i, h: (i, 0, h)),
            pl.BlockSpec((1, s, MLA_VD), lambda i, h: (i, 0, h)),
        ],
        out_specs=pl.BlockSpec((1, s, MLA_VD), lambda i, h: (i, 0, h)),
        out_shape=jax.ShapeDtypeStruct((b, s, MLA_HEADS * MLA_VD), MXU_DTYPE),
        scratch_shapes=[
            pltpu.VMEM((2, block, LANES), F32),
            pltpu.VMEM((2, block, MLA_VD + LANES), F32),
        ],
        compiler_params=_params("parallel", "parallel"),
        name="mla_attention",
    )(q, k, v)


def _tiles(seq):
    assert seq % RET_SUPER == 0, seq
    tm = 512 if seq % 512 == 0 else RET_SUPER
    mla_sub_tiles = 2 if seq % (2 * tm) == 0 else 1
    return dict(tm=tm, ret_block=tm, attn_block=tm, ff_chunk=1024, mla_sub_tiles=mla_sub_tiles)


def kernel(x, p, mix_norm, ret_w_in, ret_gn, ret_w_out, mla_w_in, mla_q_a_norm, mla_kv_a_norm, mla_w_uq,
           mla_w_ukv, mla_q_norm, mla_k_norm, mla_w_out, mlp_norm, mlp_w1, mlp_w2, ple_norm, ple_gate_w,
           ple_proj_w):
    b, s, d = x.shape
    depth = p.shape[0]
    t = b * s
    cfg = _tiles(s)
    tm = cfg["tm"]
    h = x.reshape(t, d)
    p3d = p.reshape(depth, t, -1)
    mlp_gain = mlp_norm.reshape(depth, 1, d)
    ple_gain = ple_norm.reshape(depth, 1, d)
    w1, w2, wg, wp = (w.astype(MXU_DTYPE) for w in (mlp_w1, mlp_w2, ple_gate_w, ple_proj_w))
    for i in range(depth):
        j = i // 2
        if i % 2 == 0:
            q, k, v, gate = _ret_proj(h, mix_norm[i], ret_w_in[j], ret_gn[j], s, tm)
            y = _retention(q.reshape(b, s, -1), k.reshape(b, s, -1), v.reshape(b, s, -1),
                           gate.reshape(b, s, -1), cfg["ret_block"])
            w_out = ret_w_out[j]
        else:
            q, k, v = _mla_proj(h, mix_norm[i], mla_w_in[j], mla_q_a_norm[j], mla_kv_a_norm[j], mla_w_uq[j],
                                mla_w_ukv[j], mla_q_norm[j], mla_k_norm[j], s, tm, cfg["mla_sub_tiles"])
            y = _attention(q.reshape(b, s, -1), k.reshape(b, s, -1), v.reshape(b, s, -1), cfg["attn_block"])
            w_out = mla_w_out[j]
        h = _tail(h, y.reshape(t, -1), p3d, w_out, mlp_gain, w1, w2, ple_gain, wg, wp, i, tm, cfg["ff_chunk"])
    return h.reshape(b, s, d)
```

```python
import functools

import jax
import jax.numpy as jnp
import numpy as np
from jax import lax
from jax.experimental import pallas as pl
from jax.experimental.pallas import tpu as pltpu

EPS = 1e-6
ROPE_THETA = 10000.0
CHUNK = 64
LANES = 128
MXU_WIDTH = 256

RET_HEADS = 4
RET_DK = 256
RET_DV = 512
RET_SUPER = 256

MLA_HEADS = 8
MLA_NOPE = 128
MLA_ROPE = 64
MLA_QKD = MLA_NOPE + MLA_ROPE
MLA_VD = 128
MLA_Q_RANK = 384
MLA_KV_RANK = 256
MLA_HEAD_W = 256

MXU_DTYPE = jnp.bfloat16
F32 = jnp.float32
LOG2_E = 1.4426950408889634
VMEM_LIMIT = 56 * 1024 * 1024

_NT = (((1,), (1,)), ((), ()))
_TN = (((0,), (0,)), ((), ()))


def _dot(a, b):
    return jnp.dot(a, b, preferred_element_type=F32)


def _rms_scale(x):
    return lax.rsqrt(jnp.mean(x * x, axis=-1, keepdims=True) + EPS)


def _rms_scale_lanes(x):
    return jnp.broadcast_to(_rms_scale(x), (x.shape[0], LANES))


def _row_sumsq(x, weights=None):
    if weights is None:
        weights = jnp.ones((x.shape[-1], LANES), MXU_DTYPE)
    return _dot((x * x).astype(MXU_DTYPE), weights)


def _lanes(a, width):
    return a if width == LANES else jnp.tile(a, (1, width // LANES))


def _sigmoid(x):
    return 0.5 * jnp.tanh(0.5 * x) + 0.5


def _resident(shape):
    zeros = (0,) * len(shape)
    return pl.BlockSpec(shape, lambda *_: zeros, pipeline_mode=pl.Buffered(1))


def _params(*semantics):
    return pltpu.CompilerParams(dimension_semantics=semantics, vmem_limit_bytes=VMEM_LIMIT)


def _rope_angles(seq, dim):
    inv = 1.0 / (ROPE_THETA ** (jnp.arange(0, dim, 2, dtype=F32) / dim))
    ang = jnp.arange(seq, dtype=F32)[:, None] * inv[None, :]
    return jnp.cos(ang), jnp.sin(ang)


def _ret_proj_kernel(x_ref, g_ref, w_ref, cos_ref, sin_ref, gn_ref, q_ref, k_ref, v_ref, gate_ref):
    x = x_ref[...]
    inv = _rms_scale_lanes(x)
    xg = (x * g_ref[...]).astype(MXU_DTYPE)
    inv_wide = _lanes(inv, MXU_WIDTH)
    cos = cos_ref[...] * inv
    sin = sin_ref[...] * inv
    half = RET_DK // 2
    qk_w = RET_HEADS * RET_DK
    v_w = RET_HEADS * RET_DV
    k_scale = RET_DK ** -0.5
    for lo in range(0, v_w, MXU_WIDTH):
        g = _dot(xg, w_ref[:, 2 * qk_w + v_w + lo:2 * qk_w + v_w + lo + MXU_WIDTH]) * inv_wide
        gate_ref[:, lo:lo + MXU_WIDTH] = (g * _sigmoid(g) * gn_ref[:, lo:lo + MXU_WIDTH]).astype(gate_ref.dtype)
        v_ref[:, lo:lo + MXU_WIDTH] = (
            _dot(xg, w_ref[:, 2 * qk_w + lo:2 * qk_w + lo + MXU_WIDTH]) * inv_wide).astype(v_ref.dtype)
    for h in range(RET_HEADS):
        lo = h * RET_DK
        r = _dot(xg, w_ref[:, lo:lo + RET_DK])
        x1, x2 = r[:, :half], r[:, half:]
        q_ref[:, lo:lo + half] = (x1 * cos - x2 * sin).astype(q_ref.dtype)
        q_ref[:, lo + half:lo + RET_DK] = (x2 * cos + x1 * sin).astype(q_ref.dtype)
        r = _dot(xg, w_ref[:, qk_w + lo:qk_w + lo + RET_DK])
        x1, x2 = r[:, :half], r[:, half:]
        k_ref[:, lo:lo + half] = ((x1 * cos - x2 * sin) * k_scale).astype(k_ref.dtype)
        k_ref[:, lo + half:lo + RET_DK] = ((x2 * cos + x1 * sin) * k_scale).astype(k_ref.dtype)


def _ret_proj(x2d, gain, w_in, gn_gain, seq, tm):
    t, d = x2d.shape
    qk_w = RET_HEADS * RET_DK
    v_w = RET_HEADS * RET_DV
    cos, sin = _rope_angles(seq, RET_DK)
    pos_blocks = seq // tm
    row = lambda i: (i, 0)
    pos = lambda i: (i % pos_blocks, 0)
    return pl.pallas_call(
        _ret_proj_kernel,
        grid=(t // tm,),
        in_specs=[
            pl.BlockSpec((tm, d), row),
            _resident((1, d)),
            _resident(w_in.shape),
            pl.BlockSpec((tm, RET_DK // 2), pos),
            pl.BlockSpec((tm, RET_DK // 2), pos),
            _resident((1, v_w)),
        ],
        out_specs=[
            pl.BlockSpec((tm, qk_w), row),
            pl.BlockSpec((tm, qk_w), row),
            pl.BlockSpec((tm, v_w), row),
            pl.BlockSpec((tm, v_w), row),
        ],
        out_shape=[
            jax.ShapeDtypeStruct((t, qk_w), MXU_DTYPE),
            jax.ShapeDtypeStruct((t, qk_w), MXU_DTYPE),
            jax.ShapeDtypeStruct((t, v_w), MXU_DTYPE),
            jax.ShapeDtypeStruct((t, v_w), MXU_DTYPE),
        ],
        compiler_params=_params("parallel"),
        name="ret_proj",
    )(x2d, gain.reshape(1, d), w_in.astype(MXU_DTYPE), cos, sin, gn_gain.reshape(1, v_w))


def _retention_kernel(q_ref, k_ref, v_ref, gate_ref, decay_ref, qd_ref, kd_ref, cd_ref, y_ref, state_ref,
                      *, groups):
    @pl.when(pl.program_id(1) == 0)
    def _():
        state_ref[...] = jnp.zeros_like(state_ref)

    def matmuls(c, h):
        rows = slice(c * RET_SUPER, (c + 1) * RET_SUPER)
        qk_cols = slice(h * RET_DK, (h + 1) * RET_DK)
        v_cols = slice(h * RET_DV, (h + 1) * RET_DV)
        q = q_ref[0, rows, qk_cols]
        k = k_ref[0, rows, qk_cols]
        v = v_ref[0, rows, v_cols]
        state = state_ref[h]
        scores = lax.dot_general(q, k, _NT, preferred_element_type=F32) * decay_ref[h]
        lhs = jnp.concatenate([q * qd_ref[h], scores.astype(MXU_DTYPE)], axis=1)
        rhs = jnp.concatenate([state.astype(MXU_DTYPE), v], axis=0)
        out = _dot(lhs, rhs)
        state_ref[h] = state * cd_ref[h] + lax.dot_general(k * kd_ref[h], v, _TN, preferred_element_type=F32)
        return out

    def finish(c, h, out):
        rows = slice(c * RET_SUPER, (c + 1) * RET_SUPER)
        v_cols = slice(h * RET_DV, (h + 1) * RET_DV)
        gate = gate_ref[0, rows, v_cols].astype(F32)
        y_ref[0, rows, v_cols] = (gate * (out * _rms_scale(out))).astype(y_ref.dtype)

    units = [(c, h) for c in range(groups) for h in range(RET_HEADS)]
    ahead = matmuls(*units[0])
    for u, unit in enumerate(units):
        out = ahead
        if u + 1 < len(units):
            ahead = matmuls(*units[u + 1])
        finish(*unit, out)


def _retention_tables():
    log_gamma = np.log(1.0 - 2.0 ** (-5.0 - np.arange(RET_HEADS, dtype=np.float64)))
    idx = np.arange(RET_SUPER, dtype=np.float64)
    dist = np.abs(idx[:, None] - idx[None, :])
    chunk = np.arange(RET_SUPER) // CHUNK
    visible = chunk[None, :] <= chunk[:, None]
    decay = np.where(visible[None], np.exp(log_gamma[:, None, None] * dist), 0.0)
    q_decay = np.exp(log_gamma[:, None] * (idx + 1.0))
    k_decay = np.exp(log_gamma[:, None] * (RET_SUPER - 1.0 - idx))
    group_decay = np.exp(log_gamma * RET_SUPER)
    qd = np.broadcast_to(q_decay[:, :, None], (RET_HEADS, RET_SUPER, RET_DK))
    kd = np.broadcast_to(k_decay[:, :, None], (RET_HEADS, RET_SUPER, RET_DK))
    cd = np.broadcast_to(group_decay[:, None, None], (RET_HEADS, 1, RET_DV))
    decay, qd, kd, cd = (jnp.asarray(np.ascontiguousarray(a), F32) for a in (decay, qd, kd, cd))
    return decay, qd.astype(MXU_DTYPE), kd.astype(MXU_DTYPE), cd


def _retention(q, k, v, gate, block):
    b, s, _ = q.shape
    qk_w = RET_HEADS * RET_DK
    v_w = RET_HEADS * RET_DV
    decay, qd, kd, cd = _retention_tables()
    tok = lambda i, j: (i, j, 0)
    return pl.pallas_call(
        functools.partial(_retention_kernel, groups=block // RET_SUPER),
        grid=(b, s // block),
        in_specs=[
            pl.BlockSpec((1, block, qk_w), tok),
            pl.BlockSpec((1, block, qk_w), tok),
            pl.BlockSpec((1, block, v_w), tok),
            pl.BlockSpec((1, block, v_w), tok),
            _resident(decay.shape),
            _resident(qd.shape),
            _resident(kd.shape),
            _resident(cd.shape),
        ],
        out_specs=pl.BlockSpec((1, block, v_w), tok),
        out_shape=jax.ShapeDtypeStruct((b, s, v_w), MXU_DTYPE),
        scratch_shapes=[pltpu.VMEM((RET_HEADS, RET_DK, RET_DV), F32)],
        compiler_params=_params("parallel", "arbitrary"),
        name="retention",
    )(q, k, v, gate, decay, qd, kd, cd)


def _tail_kernel(h_ref, y_ref, p_ref, wo_ref, mg_ref, w1_ref, w2_ref, pg_ref, wg_ref, wp_ref, o_ref,
                 *, ff_chunk):
    h = h_ref[...] + _dot(y_ref[...], wo_ref[...])
    d = h.shape[-1]
    inv = _rms_scale_lanes(h)
    hg = (h * mg_ref[...]).astype(MXU_DTYPE)
    d_ff = w1_ref.shape[1]
    mlp = None
    for c in range(d_ff // ff_chunk):
        cols = slice(c * ff_chunk, (c + 1) * ff_chunk)
        a = jnp.square(jnp.maximum(_dot(hg, w1_ref[:, cols]), 0.0))
        part = _dot(a.astype(MXU_DTYPE), w2_ref[cols, :])
        mlp = part if mlp is None else mlp + part
    h = h + mlp * _lanes(inv * inv, d)
    inv = _rms_scale_lanes(h)
    hg = (h * pg_ref[...]).astype(MXU_DTYPE)
    gate = _sigmoid(_dot(hg, wg_ref[...]) * _lanes(inv, d))
    emb = _dot(p_ref[...].astype(MXU_DTYPE), wp_ref[...])
    o_ref[...] = h + gate * emb


def _layer_of(shape, layer):
    index = (layer,) + (0,) * (len(shape) - 1)
    return pl.BlockSpec((None,) + tuple(shape[1:]), lambda *_: index, pipeline_mode=pl.Buffered(1))


def _tail(h2d, y2d, p3d, w_out, mlp_gain, w1, w2, ple_gain, wg, wp, layer, tm, ff_chunk):
    t, d = h2d.shape
    row = lambda i: (i, 0)
    return pl.pallas_call(
        functools.partial(_tail_kernel, ff_chunk=ff_chunk),
        grid=(t // tm,),
        in_specs=[
            pl.BlockSpec((tm, d), row),
            pl.BlockSpec((tm, y2d.shape[1]), row),
            pl.BlockSpec((None, tm, p3d.shape[2]), lambda i: (layer, i, 0)),
            _resident(w_out.shape),
            _layer_of(mlp_gain.shape, layer),
            _layer_of(w1.shape, layer),
            _layer_of(w2.shape, layer),
            _layer_of(ple_gain.shape, layer),
            _layer_of(wg.shape, layer),
            _layer_of(wp.shape, layer),
        ],
        out_specs=pl.BlockSpec((tm, d), row),
        out_shape=jax.ShapeDtypeStruct((t, d), F32),
        compiler_params=_params("parallel"),
        name="layer_tail",
    )(h2d, y2d, p3d, w_out.astype(MXU_DTYPE), mlp_gain, w1, w2, ple_gain, wg, wp)


def _mla_proj_kernel(x_ref, g_ref, win_ref, qa_ref, kva_ref, wuq_ref, wukv_ref, qn_ref, kn_ref,
                     qrope_ref, krope_ref, q_ref, k_ref, v_ref, *, sub_tiles):
    tm = x_ref.shape[0] // sub_tiles
    d = x_ref.shape[-1]
    lane = lax.broadcasted_iota(jnp.int32, (1, LANES), 1)
    first_half = (lane < MLA_ROPE).astype(F32)
    softmax_scale = MLA_QKD ** -0.5 * LOG2_E
    qn = qn_ref[...]
    kn = kn_ref[...]
    ones = jnp.ones((MLA_NOPE, LANES), MXU_DTYPE)
    halves = jnp.full((LANES, LANES), 0.5, MXU_DTYPE)
    head_weights = jnp.concatenate([ones, halves], axis=0)

    def down(rows):
        x = x_ref[rows, :]
        inv = lax.rsqrt(_row_sumsq(x) * (1.0 / d) + EPS)
        proj = _dot((x * g_ref[...]).astype(MXU_DTYPE), win_ref[...])
        proj = proj * _lanes(inv, proj.shape[-1])
        c_q = proj[:, :MLA_Q_RANK]
        c_kv = proj[:, MLA_Q_RANK:MLA_Q_RANK + MLA_KV_RANK]
        k_r = proj[:, MLA_Q_RANK + MLA_KV_RANK:]
        inv = lax.rsqrt(_row_sumsq(c_q) * (1.0 / MLA_Q_RANK) + EPS)
        c_q = (c_q * _lanes(inv, MLA_Q_RANK) * qa_ref[...]).astype(MXU_DTYPE)
        inv = lax.rsqrt(_row_sumsq(c_kv) * (1.0 / MLA_KV_RANK) + EPS)
        c_kv = (c_kv * _lanes(inv, MLA_KV_RANK) * kva_ref[...]).astype(MXU_DTYPE)
        k_rot = k_r * krope_ref[rows, :]
        k_rot = (k_rot + pltpu.roll(k_rot, MLA_ROPE, 1)) * first_half
        return c_q, c_kv, k_rot, _row_sumsq(k_r, halves)

    def heads(rows, c_q, c_kv, k_rot, k_r_ss):
        def up_project(h):
            cols = slice(h * MLA_HEAD_W, (h + 1) * MLA_HEAD_W)
            return _dot(c_q, wuq_ref[:, cols]), _dot(c_kv, wukv_ref[:, cols])

        ahead = up_project(0)
        for h in range(MLA_HEADS):
            lo = h * MLA_HEAD_W
            qh, kvh = ahead
            if h + 1 < MLA_HEADS:
                ahead = up_project(h + 1)
            q_nope, q_r = qh[:, :MLA_NOPE], qh[:, MLA_NOPE:]
            inv = lax.rsqrt(_row_sumsq(qh, head_weights) * (1.0 / MLA_QKD) + EPS) * softmax_scale
            q_rot = q_r * qrope_ref[rows, :]
            q_rot = q_rot + pltpu.roll(q_rot, MLA_ROPE, 1)
            q_ref[rows, lo:lo + MLA_NOPE] = (q_nope * inv * qn).astype(q_ref.dtype)
            q_ref[rows, lo + MLA_NOPE:lo + MLA_HEAD_W] = (q_rot * inv).astype(q_ref.dtype)

            k_nope, v = kvh[:, :MLA_NOPE], kvh[:, MLA_NOPE:]
            inv = lax.rsqrt((_row_sumsq(k_nope, ones) + k_r_ss) * (1.0 / MLA_QKD) + EPS)
            k_ref[rows, lo:lo + MLA_NOPE] = (k_nope * inv * kn).astype(k_ref.dtype)
            k_ref[rows, lo + MLA_NOPE:lo + MLA_HEAD_W] = (k_rot * inv).astype(k_ref.dtype)
            v_ref[rows, h * MLA_VD:(h + 1) * MLA_VD] = v.astype(v_ref.dtype)

    tiles = [slice(i * tm, (i + 1) * tm) for i in range(sub_tiles)]
    projected = [down(rows) for rows in tiles]
    for rows, operands in zip(tiles, projected):
        heads(rows, *operands)


def _rotate_half_cols(w):
    half = w.shape[-1] // 2
    return jnp.concatenate([-w[..., half:], w[..., :half]], axis=-1)


def _rope_gain_table(seq, gain_rope):
    cos, sin = _rope_angles(seq, MLA_ROPE)
    base = jnp.concatenate([cos, cos, sin, sin], axis=-1)
    half = MLA_ROPE // 2
    gains = jnp.concatenate([gain_rope, gain_rope[half:], gain_rope[:half]])
    return base * gains[None, :]


def _mla_proj(x2d, gain, w_in, q_a_gain, kv_a_gain, w_uq, w_ukv, q_gain, k_gain, seq, tm, sub_tiles):
    t, d = x2d.shape
    kr_lo = MLA_Q_RANK + MLA_KV_RANK
    w_in_ext = jnp.concatenate([w_in, _rotate_half_cols(w_in[:, kr_lo:])], axis=1)
    w_uq_h = w_uq.reshape(MLA_Q_RANK, MLA_HEADS, MLA_QKD)
    w_uq_ext = jnp.concatenate([w_uq_h, _rotate_half_cols(w_uq_h[..., MLA_NOPE:])], axis=-1)
    w_uq_ext = w_uq_ext.reshape(MLA_Q_RANK, MLA_HEADS * MLA_HEAD_W)
    q_rope = _rope_gain_table(seq, q_gain[MLA_NOPE:])
    k_rope = _rope_gain_table(seq, k_gain[MLA_NOPE:])
    qk_w = MLA_HEADS * MLA_HEAD_W
    v_w = MLA_HEADS * MLA_VD
    block = sub_tiles * tm
    pos_blocks = seq // block
    row = lambda i: (i, 0)
    pos = lambda i: (i % pos_blocks, 0)
    return pl.pallas_call(
        functools.partial(_mla_proj_kernel, sub_tiles=sub_tiles),
        grid=(t // block,),
        in_specs=[
            pl.BlockSpec((block, d), row),
            _resident((1, d)),
            _resident(w_in_ext.shape),
            _resident((1, MLA_Q_RANK)),
            _resident((1, MLA_KV_RANK)),
            _resident(w_uq_ext.shape),
            _resident(w_ukv.shape),
            _resident((1, MLA_NOPE)),
            _resident((1, MLA_NOPE)),
            pl.BlockSpec((block, LANES), pos),
            pl.BlockSpec((block, LANES), pos),
        ],
        out_specs=[
            pl.BlockSpec((block, qk_w), row),
            pl.BlockSpec((block, qk_w), row),
            pl.BlockSpec((block, v_w), row),
        ],
        out_shape=[
            jax.ShapeDtypeStruct((t, qk_w), MXU_DTYPE),
            jax.ShapeDtypeStruct((t, qk_w), MXU_DTYPE),
            jax.ShapeDtypeStruct((t, v_w), MXU_DTYPE),
        ],
        compiler_params=_params("parallel"),
        name="mla_proj",
    )(x2d, gain.reshape(1, d), w_in_ext.astype(MXU_DTYPE), q_a_gain.reshape(1, -1), kv_a_gain.reshape(1, -1),
      w_uq_ext.astype(MXU_DTYPE), w_ukv.astype(MXU_DTYPE), q_gain[:MLA_NOPE].reshape(1, -1),
      k_gain[:MLA_NOPE].reshape(1, -1), q_rope, k_rope)


def _attn_kernel(q_ref, k_ref, v_ref, o_ref, m_ref, acc_ref, *, block):
    n_blocks = q_ref.shape[1] // block
    half = block // 2

    units = []
    for qb in range(n_blocks):
        q0 = qb * block
        units += [((q0, block), (kb * block, block)) for kb in range(qb)]
        units += [((q0, half), (q0, half)), ((q0 + half, half), (q0, block))]

    def span(start_size):
        return slice(start_size[0], start_size[0] + start_size[1])

    def scores(q_rows, k_rows):
        return lax.dot_general(q_ref[0, span(q_rows), :], k_ref[0, span(k_rows), :], _NT,
                               preferred_element_type=F32)

    def update(s, q_rows, k_rows):
        (q0, nq), (k0, nk) = q_rows, k_rows
        slot = (q0 // block) % 2
        local = slice(q0 % block, q0 % block + nq)
        first = k0 == 0
        last = k0 + nk == q0 + nq
        if last:
            q_chunk = (q0 + lax.broadcasted_iota(jnp.int32, s.shape, 0)) // CHUNK
            k_chunk = (k0 + lax.broadcasted_iota(jnp.int32, s.shape, 1)) // CHUNK
            s = jnp.where(k_chunk <= q_chunk, s, -1e30)
        m_new = jnp.broadcast_to(jnp.max(s, axis=-1, keepdims=True), (nq, LANES))
        if not first:
            m_prev = m_ref[slot, local, :]
            m_new = jnp.maximum(m_prev, m_new)
        p = jnp.exp2(s - jnp.tile(m_new, (1, nk // LANES)))
        ones = jnp.ones((nk, LANES), MXU_DTYPE)
        acc = _dot(p.astype(MXU_DTYPE), jnp.concatenate([v_ref[0, span(k_rows), :], ones], axis=1))
        if not first:
            acc = jnp.tile(jnp.exp2(m_prev - m_new), (1, 2)) * acc_ref[slot, local, :] + acc
        if last:
            o_ref[0, span(q_rows), :] = (acc[:, :MLA_VD] / acc[:, MLA_VD:]).astype(o_ref.dtype)
        else:
            m_ref[slot, local, :] = m_new
            acc_ref[slot, local, :] = acc

    s_next = scores(*units[0])
    for u, unit in enumerate(units):
        s = s_next
        if u + 1 < len(units):
            s_next = scores(*units[u + 1])
        update(s, *unit)


def _attention(q, k, v, block):
    b, s, _ = q.shape
    return pl.pallas_call(
        functools.partial(_attn_kernel, block=block),
        grid=(b, MLA_HEADS),
        in_specs=[
            pl.BlockSpec((1, s, MLA_HEAD_W), lambda i, h: (i, 0, h)),
            pl.BlockSpec((1, s, MLA_HEAD_W), lambda i, h: (i, 0, h)),
            pl.BlockSpec((1, s, MLA_VD), lambda i, h: (i, 0, h)),
        ],
        out_specs=pl.BlockSpec((1, s, MLA_VD), lambda i, h: (i, 0, h)),
        out_shape=jax.ShapeDtypeStruct((b, s, MLA_HEADS * MLA_VD), MXU_DTYPE),
        scratch_shapes=[
            pltpu.VMEM((2, block, LANES), F32),
            pltpu.VMEM((2, block, MLA_VD + LANES), F32),
        ],
        compiler_params=_params("parallel", "parallel"),
        name="mla_attention",
    )(q, k, v)


def _tiles(seq):
    assert seq % RET_SUPER == 0, seq
    tm = 512 if seq % 512 == 0 else RET_SUPER
    mla_sub_tiles = 2 if seq % (2 * tm) == 0 else 1
    return dict(tm=tm, ret_block=tm, attn_block=tm, ff_chunk=1024, mla_sub_tiles=mla_sub_tiles)


def kernel(x, p, mix_norm, ret_w_in, ret_gn, ret_w_out, mla_w_in, mla_q_a_norm, mla_kv_a_norm, mla_w_uq,
           mla_w_ukv, mla_q_norm, mla_k_norm, mla_w_out, mlp_norm, mlp_w1, mlp_w2, ple_norm, ple_gate_w,
           ple_proj_w):
    b, s, d = x.shape
    depth = p.shape[0]
    t = b * s
    cfg = _tiles(s)
    tm = cfg["tm"]
    h = x.reshape(t, d)
    p3d = p.reshape(depth, t, -1)
    mlp_gain = mlp_norm.reshape(depth, 1, d)
    ple_gain = ple_norm.reshape(depth, 1, d)
    w1, w2, wg, wp = (w.astype(MXU_DTYPE) for w in (mlp_w1, mlp_w2, ple_gate_w, ple_proj_w))
    for i in range(depth):
        j = i // 2
        if i % 2 == 0:
            q, k, v, gate = _ret_proj(h, mix_norm[i], ret_w_in[j], ret_gn[j], s, tm)
            y = _retention(q.reshape(b, s, -1), k.reshape(b, s, -1), v.reshape(b, s, -1),
                           gate.reshape(b, s, -1), cfg["ret_block"])
            w_out = ret_w_out[j]
        else:
            q, k, v = _mla_proj(h, mix_norm[i], mla_w_in[j], mla_q_a_norm[j], mla_kv_a_norm[j], mla_w_uq[j],
                                mla_w_ukv[j], mla_q_norm[j], mla_k_norm[j], s, tm, cfg["mla_sub_tiles"])
            y = _attention(q.reshape(b, s, -1), k.reshape(b, s, -1), v.reshape(b, s, -1), cfg["attn_block"])
            w_out = mla_w_out[j]
        h = _tail(h, y.reshape(t, -1), p3d, w_out, mlp_gain, w1, w2, ple_gain, wg, wp, i, tm, cfg["ff_chunk"])
    return h.reshape(b, s, d)
```

```python
import functools

import jax
import jax.numpy as jnp
import numpy as np
from jax import lax
from jax.experimental import pallas as pl
from jax.experimental.pallas import tpu as pltpu

EPS = 1e-6
ROPE_THETA = 10000.0
CHUNK = 64
LANES = 128
MXU_WIDTH = 256

RET_HEADS = 4
RET_DK = 256
RET_DV = 512
RET_SUPER = 256

MLA_HEADS = 8
MLA_NOPE = 128
MLA_ROPE = 64
MLA_QKD = MLA_NOPE + MLA_ROPE
MLA_VD = 128
MLA_Q_RANK = 384
MLA_KV_RANK = 256
MLA_HEAD_W = 256

MXU_DTYPE = jnp.bfloat16
F32 = jnp.float32
LOG2_E = 1.4426950408889634
VMEM_LIMIT = 56 * 1024 * 1024

_NT = (((1,), (1,)), ((), ()))
_TN = (((0,), (0,)), ((), ()))


def _dot(a, b):
    return jnp.dot(a, b, preferred_element_type=F32)


def _rms_scale(x):
    return lax.rsqrt(jnp.mean(x * x, axis=-1, keepdims=True) + EPS)


def _rms_scale_lanes(x):
    return jnp.broadcast_to(_rms_scale(x), (x.shape[0], LANES))


def _row_sumsq(x, weights=None):
    if weights is None:
        weights = jnp.ones((x.shape[-1], LANES), MXU_DTYPE)
    return _dot((x * x).astype(MXU_DTYPE), weights)


def _lanes(a, width):
    return a if width == LANES else jnp.tile(a, (1, width // LANES))


def _sigmoid(x):
    return 0.5 * jnp.tanh(0.5 * x) + 0.5


def _resident(shape):
    zeros = (0,) * len(shape)
    return pl.BlockSpec(shape, lambda *_: zeros, pipeline_mode=pl.Buffered(1))


def _params(*semantics, fuse_inputs=None):
    return pltpu.CompilerParams(dimension_semantics=semantics, vmem_limit_bytes=VMEM_LIMIT,
                                allow_input_fusion=fuse_inputs)


def _rope_angles(seq, dim):
    inv = 1.0 / (ROPE_THETA ** (jnp.arange(0, dim, 2, dtype=F32) / dim))
    ang = jnp.arange(seq, dtype=F32)[:, None] * inv[None, :]
    return jnp.cos(ang), jnp.sin(ang)


def _ret_proj_kernel(x_ref, g_ref, w_ref, cos_ref, sin_ref, gn_ref, q_ref, k_ref, v_ref, gate_ref):
    x = x_ref[...]
    inv = _rms_scale_lanes(x)
    xg = (x * g_ref[...]).astype(MXU_DTYPE)
    inv_wide = _lanes(inv, MXU_WIDTH)
    cos = cos_ref[...] * inv
    sin = sin_ref[...] * inv
    half = RET_DK // 2
    qk_w = RET_HEADS * RET_DK
    v_w = RET_HEADS * RET_DV
    k_scale = RET_DK ** -0.5
    for lo in range(0, v_w, MXU_WIDTH):
        g = _dot(xg, w_ref[:, 2 * qk_w + v_w + lo:2 * qk_w + v_w + lo + MXU_WIDTH]) * inv_wide
        gate_ref[:, lo:lo + MXU_WIDTH] = (g * _sigmoid(g) * gn_ref[:, lo:lo + MXU_WIDTH]).astype(gate_ref.dtype)
        v_ref[:, lo:lo + MXU_WIDTH] = (
            _dot(xg, w_ref[:, 2 * qk_w + lo:2 * qk_w + lo + MXU_WIDTH]) * inv_wide).astype(v_ref.dtype)
    for h in range(RET_HEADS):
        lo = h * RET_DK
        r = _dot(xg, w_ref[:, lo:lo + RET_DK])
        x1, x2 = r[:, :half], r[:, half:]
        q_ref[:, lo:lo + half] = (x1 * cos - x2 * sin).astype(q_ref.dtype)
        q_ref[:, lo + half:lo + RET_DK] = (x2 * cos + x1 * sin).astype(q_ref.dtype)
        r = _dot(xg, w_ref[:, qk_w + lo:qk_w + lo + RET_DK])
        x1, x2 = r[:, :half], r[:, half:]
        k_ref[:, lo:lo + half] = ((x1 * cos - x2 * sin) * k_scale).astype(k_ref.dtype)
        k_ref[:, lo + half:lo + RET_DK] = ((x2 * cos + x1 * sin) * k_scale).astype(k_ref.dtype)


def _ret_proj(x2d, gain, w_in, gn_gain, seq, tm):
    t, d = x2d.shape
    qk_w = RET_HEADS * RET_DK
    v_w = RET_HEADS * RET_DV
    cos, sin = _rope_angles(seq, RET_DK)
    pos_blocks = seq // tm
    row = lambda i: (i, 0)
    pos = lambda i: (i % pos_blocks, 0)
    return pl.pallas_call(
        _ret_proj_kernel,
        grid=(t // tm,),
        in_specs=[
            pl.BlockSpec((tm, d), row),
            _resident((1, d)),
            _resident(w_in.shape),
            pl.BlockSpec((tm, RET_DK // 2), pos),
            pl.BlockSpec((tm, RET_DK // 2), pos),
            _resident((1, v_w)),
        ],
        out_specs=[
            pl.BlockSpec((tm, qk_w), row),
            pl.BlockSpec((tm, qk_w), row),
            pl.BlockSpec((tm, v_w), row),
            pl.BlockSpec((tm, v_w), row),
        ],
        out_shape=[
            jax.ShapeDtypeStruct((t, qk_w), MXU_DTYPE),
            jax.ShapeDtypeStruct((t, qk_w), MXU_DTYPE),
            jax.ShapeDtypeStruct((t, v_w), MXU_DTYPE),
            jax.ShapeDtypeStruct((t, v_w), MXU_DTYPE),
        ],
        compiler_params=_params("parallel", fuse_inputs=[False, False, True, False, False, False]),
        name="ret_proj",
    )(x2d, gain.reshape(1, d), w_in.astype(MXU_DTYPE), cos, sin, gn_gain.reshape(1, v_w))


def _retention_kernel(q_ref, k_ref, v_ref, gate_ref, decay_ref, qd_ref, kd_ref, cd_ref, y_ref, state_ref,
                      *, groups):
    @pl.when(pl.program_id(1) == 0)
    def _():
        state_ref[...] = jnp.zeros_like(state_ref)

    def matmuls(c, h):
        rows = slice(c * RET_SUPER, (c + 1) * RET_SUPER)
        qk_cols = slice(h * RET_DK, (h + 1) * RET_DK)
        v_cols = slice(h * RET_DV, (h + 1) * RET_DV)
        q = q_ref[0, rows, qk_cols]
        k = k_ref[0, rows, qk_cols]
        v = v_ref[0, rows, v_cols]
        state = state_ref[h]
        scores = lax.dot_general(q, k, _NT, preferred_element_type=F32) * decay_ref[h]
        lhs = jnp.concatenate([q * qd_ref[h], scores.astype(MXU_DTYPE)], axis=1)
        rhs = jnp.concatenate([state.astype(MXU_DTYPE), v], axis=0)
        out = _dot(lhs, rhs)
        state_ref[h] = state * cd_ref[h] + lax.dot_general(k * kd_ref[h], v, _TN, preferred_element_type=F32)
        return out

    def finish(c, h, out):
        rows = slice(c * RET_SUPER, (c + 1) * RET_SUPER)
        v_cols = slice(h * RET_DV, (h + 1) * RET_DV)
        gate = gate_ref[0, rows, v_cols].astype(F32)
        y_ref[0, rows, v_cols] = (gate * (out * _rms_scale(out))).astype(y_ref.dtype)

    units = [(c, h) for c in range(groups) for h in range(RET_HEADS)]
    ahead = matmuls(*units[0])
    for u, unit in enumerate(units):
        out = ahead
        if u + 1 < len(units):
            ahead = matmuls(*units[u + 1])
        finish(*unit, out)


def _retention_tables():
    log_gamma = np.log(1.0 - 2.0 ** (-5.0 - np.arange(RET_HEADS, dtype=np.float64)))
    idx = np.arange(RET_SUPER, dtype=np.float64)
    dist = np.abs(idx[:, None] - idx[None, :])
    chunk = np.arange(RET_SUPER) // CHUNK
    visible = chunk[None, :] <= chunk[:, None]
    decay = np.where(visible[None], np.exp(log_gamma[:, None, None] * dist), 0.0)
    q_decay = np.exp(log_gamma[:, None] * (idx + 1.0))
    k_decay = np.exp(log_gamma[:, None] * (RET_SUPER - 1.0 - idx))
    group_decay = np.exp(log_gamma * RET_SUPER)
    qd = np.broadcast_to(q_decay[:, :, None], (RET_HEADS, RET_SUPER, RET_DK))
    kd = np.broadcast_to(k_decay[:, :, None], (RET_HEADS, RET_SUPER, RET_DK))
    cd = np.broadcast_to(group_decay[:, None, None], (RET_HEADS, 1, RET_DV))
    decay, qd, kd, cd = (jnp.asarray(np.ascontiguousarray(a), F32) for a in (decay, qd, kd, cd))
    return decay, qd.astype(MXU_DTYPE), kd.astype(MXU_DTYPE), cd


def _retention(q, k, v, gate, block):
    b, s, _ = q.shape
    qk_w = RET_HEADS * RET_DK
    v_w = RET_HEADS * RET_DV
    decay, qd, kd, cd = _retention_tables()
    tok = lambda i, j: (i, j, 0)
    return pl.pallas_call(
        functools.partial(_retention_kernel, groups=block // RET_SUPER),
        grid=(b, s // block),
        in_specs=[
            pl.BlockSpec((1, block, qk_w), tok),
            pl.BlockSpec((1, block, qk_w), tok),
            pl.BlockSpec((1, block, v_w), tok),
            pl.BlockSpec((1, block, v_w), tok),
            _resident(decay.shape),
            _resident(qd.shape),
            _resident(kd.shape),
            _resident(cd.shape),
        ],
        out_specs=pl.BlockSpec((1, block, v_w), tok),
        out_shape=jax.ShapeDtypeStruct((b, s, v_w), MXU_DTYPE),
        scratch_shapes=[pltpu.VMEM((RET_HEADS, RET_DK, RET_DV), F32)],
        compiler_params=_params("parallel", "arbitrary"),
        name="retention",
    )(q, k, v, gate, decay, qd, kd, cd)


def _tail_kernel(h_ref, y_ref, p_ref, wo_ref, mg_ref, w1_ref, w2_ref, pg_ref, wg_ref, wp_ref, o_ref,
                 *, ff_chunk):
    h = h_ref[...] + _dot(y_ref[...], wo_ref[...])
    d = h.shape[-1]
    inv = _rms_scale_lanes(h)
    hg = (h * mg_ref[...]).astype(MXU_DTYPE)
    d_ff = w1_ref.shape[1]
    mlp = None
    for c in range(d_ff // ff_chunk):
        cols = slice(c * ff_chunk, (c + 1) * ff_chunk)
        a = jnp.square(jnp.maximum(_dot(hg, w1_ref[:, cols]), 0.0))
        part = _dot(a.astype(MXU_DTYPE), w2_ref[cols, :])
        mlp = part if mlp is None else mlp + part
    h = h + mlp * _lanes(inv * inv, d)
    inv = _rms_scale_lanes(h)
    hg = (h * pg_ref[...]).astype(MXU_DTYPE)
    gate = _sigmoid(_dot(hg, wg_ref[...]) * _lanes(inv, d))
    emb = _dot(p_ref[...].astype(MXU_DTYPE), wp_ref[...])
    o_ref[...] = h + gate * emb


def _layer_of(shape, layer):
    index = (layer,) + (0,) * (len(shape) - 1)
    return pl.BlockSpec((None,) + tuple(shape[1:]), lambda *_: index, pipeline_mode=pl.Buffered(1))


def _tail(h2d, y2d, p3d, w_out, mlp_gain, w1, w2, ple_gain, wg, wp, layer, tm, ff_chunk):
    t, d = h2d.shape
    row = lambda i: (i, 0)
    return pl.pallas_call(
        functools.partial(_tail_kernel, ff_chunk=ff_chunk),
        grid=(t // tm,),
        in_specs=[
            pl.BlockSpec((tm, d), row),
            pl.BlockSpec((tm, y2d.shape[1]), row),
            pl.BlockSpec((None, tm, p3d.shape[2]), lambda i: (layer, i, 0)),
            _resident(w_out.shape),
            _layer_of(mlp_gain.shape, layer),
            _layer_of(w1.shape, layer),
            _layer_of(w2.shape, layer),
            _layer_of(ple_gain.shape, layer),
            _layer_of(wg.shape, layer),
            _layer_of(wp.shape, layer),
        ],
        out_specs=pl.BlockSpec((tm, d), row),
        out_shape=jax.ShapeDtypeStruct((t, d), F32),
        compiler_params=_params(
            "parallel", fuse_inputs=[False, False, False, True, False, True, True, False, True, True]),
        name="layer_tail",
    )(h2d, y2d, p3d, w_out.astype(MXU_DTYPE), mlp_gain, w1, w2, ple_gain, wg, wp)


def _mla_proj_kernel(x_ref, g_ref, win_ref, qa_ref, kva_ref, wuq_ref, wukv_ref, qn_ref, kn_ref,
                     qrope_ref, krope_ref, q_ref, k_ref, v_ref, *, sub_tiles):
    tm = x_ref.shape[0] // sub_tiles
    d = x_ref.shape[-1]
    lane = lax.broadcasted_iota(jnp.int32, (1, LANES), 1)
    first_half = (lane < MLA_ROPE).astype(F32)
    softmax_scale = MLA_QKD ** -0.5 * LOG2_E
    qn = qn_ref[...]
    kn = kn_ref[...]
    ones = jnp.ones((MLA_NOPE, LANES), MXU_DTYPE)
    halves = jnp.full((LANES, LANES), 0.5, MXU_DTYPE)
    head_weights = jnp.concatenate([ones, halves], axis=0)

    def down(rows):
        x = x_ref[rows, :]
        inv = lax.rsqrt(_row_sumsq(x) * (1.0 / d) + EPS)
        proj = _dot((x * g_ref[...]).astype(MXU_DTYPE), win_ref[...])
        proj = proj * _lanes(inv, proj.shape[-1])
        c_q = proj[:, :MLA_Q_RANK]
        c_kv = proj[:, MLA_Q_RANK:MLA_Q_RANK + MLA_KV_RANK]
        k_r = proj[:, MLA_Q_RANK + MLA_KV_RANK:]
        inv = lax.rsqrt(_row_sumsq(c_q) * (1.0 / MLA_Q_RANK) + EPS)
        c_q = (c_q * _lanes(inv, MLA_Q_RANK) * qa_ref[...]).astype(MXU_DTYPE)
        inv = lax.rsqrt(_row_sumsq(c_kv) * (1.0 / MLA_KV_RANK) + EPS)
        c_kv = (c_kv * _lanes(inv, MLA_KV_RANK) * kva_ref[...]).astype(MXU_DTYPE)
        k_rot = k_r * krope_ref[rows, :]
        k_rot = (k_rot + pltpu.roll(k_rot, MLA_ROPE, 1)) * first_half
        return c_q, c_kv, k_rot, _row_sumsq(k_r, halves)

    def heads(rows, c_q, c_kv, k_rot, k_r_ss):
        def up_project(h):
            cols = slice(h * MLA_HEAD_W, (h + 1) * MLA_HEAD_W)
            return _dot(c_q, wuq_ref[:, cols]), _dot(c_kv, wukv_ref[:, cols])

        ahead = up_project(0)
        for h in range(MLA_HEADS):
            lo = h * MLA_HEAD_W
            qh, kvh = ahead
            if h + 1 < MLA_HEADS:
                ahead = up_project(h + 1)
            q_nope, q_r = qh[:, :MLA_NOPE], qh[:, MLA_NOPE:]
            inv = lax.rsqrt(_row_sumsq(qh, head_weights) * (1.0 / MLA_QKD) + EPS) * softmax_scale
            q_rot = q_r * qrope_ref[rows, :]
            q_rot = q_rot + pltpu.roll(q_rot, MLA_ROPE, 1)
            q_ref[rows, lo:lo + MLA_NOPE] = (q_nope * inv * qn).astype(q_ref.dtype)
            q_ref[rows, lo + MLA_NOPE:lo + MLA_HEAD_W] = (q_rot * inv).astype(q_ref.dtype)

            k_nope, v = kvh[:, :MLA_NOPE], kvh[:, MLA_NOPE:]
            inv = lax.rsqrt((_row_sumsq(k_nope, ones) + k_r_ss) * (1.0 / MLA_QKD) + EPS)
            k_ref[rows, lo:lo + MLA_NOPE] = (k_nope * inv * kn).astype(k_ref.dtype)
            k_ref[rows, lo + MLA_NOPE:lo + MLA_HEAD_W] = (k_rot * inv).astype(k_ref.dtype)
            v_ref[rows, h * MLA_VD:(h + 1) * MLA_VD] = v.astype(v_ref.dtype)

    tiles = [slice(i * tm, (i + 1) * tm) for i in range(sub_tiles)]
    projected = [down(rows) for rows in tiles]
    for rows, operands in zip(tiles, projected):
        heads(rows, *operands)


def _rotate_half_cols(w):
    half = w.shape[-1] // 2
    return jnp.concatenate([-w[..., half:], w[..., :half]], axis=-1)


def _rope_gain_table(seq, gain_rope):
    cos, sin = _rope_angles(seq, MLA_ROPE)
    base = jnp.concatenate([cos, cos, sin, sin], axis=-1)
    half = MLA_ROPE // 2
    gains = jnp.concatenate([gain_rope, gain_rope[half:], gain_rope[:half]])
    return base * gains[None, :]


def _mla_proj(x2d, gain, w_in, q_a_gain, kv_a_gain, w_uq, w_ukv, q_gain, k_gain, seq, tm, sub_tiles):
    t, d = x2d.shape
    kr_lo = MLA_Q_RANK + MLA_KV_RANK
    w_in_ext = jnp.concatenate([w_in, _rotate_half_cols(w_in[:, kr_lo:])], axis=1)
    w_uq_h = w_uq.reshape(MLA_Q_RANK, MLA_HEADS, MLA_QKD)
    w_uq_ext = jnp.concatenate([w_uq_h, _rotate_half_cols(w_uq_h[..., MLA_NOPE:])], axis=-1)
    w_uq_ext = w_uq_ext.reshape(MLA_Q_RANK, MLA_HEADS * MLA_HEAD_W)
    q_rope = _rope_gain_table(seq, q_gain[MLA_NOPE:])
    k_rope = _rope_gain_table(seq, k_gain[MLA_NOPE:])
    qk_w = MLA_HEADS * MLA_HEAD_W
    v_w = MLA_HEADS * MLA_VD
    block = sub_tiles * tm
    pos_blocks = seq // block
    row = lambda i: (i, 0)
    pos = lambda i: (i % pos_blocks, 0)
    return pl.pallas_call(
        functools.partial(_mla_proj_kernel, sub_tiles=sub_tiles),
        grid=(t // block,),
        in_specs=[
            pl.BlockSpec((block, d), row),
            _resident((1, d)),
            _resident(w_in_ext.shape),
            _resident((1, MLA_Q_RANK)),
            _resident((1, MLA_KV_RANK)),
            _resident(w_uq_ext.shape),
            _resident(w_ukv.shape),
            _resident((1, MLA_NOPE)),
            _resident((1, MLA_NOPE)),
            pl.BlockSpec((block, LANES), pos),
            pl.BlockSpec((block, LANES), pos),
        ],
        out_specs=[
            pl.BlockSpec((block, qk_w), row),
            pl.BlockSpec((block, qk_w), row),
            pl.BlockSpec((block, v_w), row),
        ],
        out_shape=[
            jax.ShapeDtypeStruct((t, qk_w), MXU_DTYPE),
            jax.ShapeDtypeStruct((t, qk_w), MXU_DTYPE),
            jax.ShapeDtypeStruct((t, v_w), MXU_DTYPE),
        ],
        compiler_params=_params("parallel"),
        name="mla_proj",
    )(x2d, gain.reshape(1, d), w_in_ext.astype(MXU_DTYPE), q_a_gain.reshape(1, -1), kv_a_gain.reshape(1, -1),
      w_uq_ext.astype(MXU_DTYPE), w_ukv.astype(MXU_DTYPE), q_gain[:MLA_NOPE].reshape(1, -1),
      k_gain[:MLA_NOPE].reshape(1, -1), q_rope, k_rope)


def _attn_kernel(q_ref, k_ref, v_ref, o_ref, m_ref, acc_ref, *, block):
    n_blocks = q_ref.shape[1] // block
    half = block // 2

    units = []
    for qb in range(n_blocks):
        q0 = qb * block
        units += [((q0, block), (kb * block, block)) for kb in range(qb)]
        units += [((q0, half), (q0, half)), ((q0 + half, half), (q0, block))]

    def span(start_size):
        return slice(start_size[0], start_size[0] + start_size[1])

    def scores(q_rows, k_rows):
        return lax.dot_general(q_ref[0, span(q_rows), :], k_ref[0, span(k_rows), :], _NT,
                               preferred_element_type=F32)

    def update(s, q_rows, k_rows):
        (q0, nq), (k0, nk) = q_rows, k_rows
        slot = (q0 // block) % 2
        local = slice(q0 % block, q0 % block + nq)
        first = k0 == 0
        last = k0 + nk == q0 + nq
        if last:
            q_chunk = (q0 + lax.broadcasted_iota(jnp.int32, s.shape, 0)) // CHUNK
            k_chunk = (k0 + lax.broadcasted_iota(jnp.int32, s.shape, 1)) // CHUNK
            s = jnp.where(k_chunk <= q_chunk, s, -1e30)
        m_new = jnp.broadcast_to(jnp.max(s, axis=-1, keepdims=True), (nq, LANES))
        if not first:
            m_prev = m_ref[slot, local, :]
            m_new = jnp.maximum(m_prev, m_new)
        p = jnp.exp2(s - jnp.tile(m_new, (1, nk // LANES)))
        ones = jnp.ones((nk, LANES), MXU_DTYPE)
        acc = _dot(p.astype(MXU_DTYPE), jnp.concatenate([v_ref[0, span(k_rows), :], ones], axis=1))
        if not first:
            acc = jnp.tile(jnp.exp2(m_prev - m_new), (1, 2)) * acc_ref[slot, local, :] + acc
        if last:
            o_ref[0, span(q_rows), :] = (acc[:, :MLA_VD] / acc[:, MLA_VD:]).astype(o_ref.dtype)
        else:
            m_ref[slot, local, :] = m_new
            acc_ref[slot, local, :] = acc

    s_next = scores(*units[0])
    for u, unit in enumerate(units):
        s = s_next
        if u + 1 < len(units):
            s_next = scores(*units[u + 1])
        update(s, *unit)


def _attention(q, k, v, block):
    b, s, _ = q.shape
    return pl.pallas_call(
        functools.partial(_attn_kernel, block=block),
        grid=(b, MLA_HEADS),
        in_specs=[
            pl.BlockSpec((1, s, MLA_HEAD_W), lambda i, h: (i, 0, h)),
            pl.BlockSpec((1, s, MLA_HEAD_W), lambda i, h: (i, 0, h)),
            pl.BlockSpec((1, s, MLA_VD), lambda i, h: (i, 0, h)),
        ],
        out_specs=pl.BlockSpec((1, s, MLA_VD), lambda i, h: (i, 0, h)),
        out_shape=jax.ShapeDtypeStruct((b, s, MLA_HEADS * MLA_VD), MXU_DTYPE),
        scratch_shapes=[
            pltpu.VMEM((2, block, LANES), F32),
            pltpu.VMEM((2, block, MLA_VD + LANES), F32),
        ],
        compiler_params=_params("parallel", "parallel"),
        name="mla_attention",
    )(q, k, v)


def _tiles(seq):
    assert seq % RET_SUPER == 0, seq
    tm = 512 if seq % 512 == 0 else RET_SUPER
    mla_sub_tiles = 2 if seq % (2 * tm) == 0 else 1
    ret_block = 2 * tm if seq % (2 * tm) == 0 else tm
    return dict(tm=tm, ret_block=ret_block, attn_block=tm, ff_chunk=1024, mla_sub_tiles=mla_sub_tiles)


def kernel(x, p, mix_norm, ret_w_in, ret_gn, ret_w_out, mla_w_in, mla_q_a_norm, mla_kv_a_norm, mla_w_uq,
           mla_w_ukv, mla_q_norm, mla_k_norm, mla_w_out, mlp_norm, mlp_w1, mlp_w2, ple_norm, ple_gate_w,
           ple_proj_w):
    b, s, d = x.shape
    depth = p.shape[0]
    t = b * s
    cfg = _tiles(s)
    tm = cfg["tm"]
    h = x.reshape(t, d)
    p3d = p.reshape(depth, t, -1)
    mlp_gain = mlp_norm.reshape(depth, 1, d)
    ple_gain = ple_norm.reshape(depth, 1, d)
    w1, w2, wg, wp = (w.astype(MXU_DTYPE) for w in (mlp_w1, mlp_w2, ple_gate_w, ple_proj_w))
    for i in range(depth):
        j = i // 2
        if i % 2 == 0:
            q, k, v, gate = _ret_proj(h, mix_norm[i], ret_w_in[j], ret_gn[j], s, tm)
            y = _retention(q.reshape(b, s, -1), k.reshape(b, s, -1), v.reshape(b, s, -1),
                           gate.reshape(b, s, -1), cfg["ret_block"])
            w_out = ret_w_out[j]
        else:
            q, k, v = _mla_proj(h, mix_norm[i], mla_w_in[j], mla_q_a_norm[j], mla_kv_a_norm[j], mla_w_uq[j],
                                mla_w_ukv[j], mla_q_norm[j], mla_k_norm[j], s, tm, cfg["mla_sub_tiles"])
            y = _attention(q.reshape(b, s, -1), k.reshape(b, s, -1), v.reshape(b, s, -1), cfg["attn_block"])
            w_out = mla_w_out[j]
        h = _tail(h, y.reshape(t, -1), p3d, w_out, mlp_gain, w1, w2, ple_gain, wg, wp, i, tm, cfg["ff_chunk"])
    return h.reshape(b, s, d)
```

```python
import functools

import jax
import jax.numpy as jnp
import numpy as np
from jax import lax
from jax.experimental import pallas as pl
from jax.experimental.pallas import tpu as pltpu

EPS = 1e-6
ROPE_THETA = 10000.0
CHUNK = 64
LANES = 128
MXU_WIDTH = 256

RET_HEADS = 4
RET_DK = 256
RET_DV = 512
RET_SUPER = 256

MLA_HEADS = 8
MLA_NOPE = 128
MLA_ROPE = 64
MLA_QKD = MLA_NOPE + MLA_ROPE
MLA_VD = 128
MLA_Q_RANK = 384
MLA_KV_RANK = 256
MLA_HEAD_W = 256

MXU_DTYPE = jnp.bfloat16
F32 = jnp.float32
LOG2_E = 1.4426950408889634
VMEM_LIMIT = 56 * 1024 * 1024

_NT = (((1,), (1,)), ((), ()))
_TN = (((0,), (0,)), ((), ()))


def _dot(a, b):
    return jnp.dot(a, b, preferred_element_type=F32)


def _rms_scale(x):
    return lax.rsqrt(jnp.mean(x * x, axis=-1, keepdims=True) + EPS)


def _rms_scale_lanes(x):
    return jnp.broadcast_to(_rms_scale(x), (x.shape[0], LANES))


def _row_sumsq(x, weights=None):
    if weights is None:
        weights = jnp.ones((x.shape[-1], LANES), MXU_DTYPE)
    return _dot((x * x).astype(MXU_DTYPE), weights)


def _lanes(a, width):
    return a if width == LANES else jnp.tile(a, (1, width // LANES))


def _sigmoid(x):
    return 0.5 * jnp.tanh(0.5 * x) + 0.5


def _resident(shape):
    zeros = (0,) * len(shape)
    return pl.BlockSpec(shape, lambda *_: zeros, pipeline_mode=pl.Buffered(1))


def _params(*semantics):
    return pltpu.CompilerParams(dimension_semantics=semantics, vmem_limit_bytes=VMEM_LIMIT)


def _rope_angles(seq, dim):
    inv = 1.0 / (ROPE_THETA ** (jnp.arange(0, dim, 2, dtype=F32) / dim))
    ang = jnp.arange(seq, dtype=F32)[:, None] * inv[None, :]
    return jnp.cos(ang), jnp.sin(ang)


def _ret_proj_kernel(x_ref, g_ref, w_ref, cos_ref, sin_ref, gn_ref, q_ref, k_ref, v_ref, gate_ref):
    x = x_ref[...]
    inv = _rms_scale_lanes(x)
    xg = (x * g_ref[...]).astype(MXU_DTYPE)
    inv_wide = _lanes(inv, MXU_WIDTH)
    cos = cos_ref[...] * inv
    sin = sin_ref[...] * inv
    half = RET_DK // 2
    qk_w = RET_HEADS * RET_DK
    v_w = RET_HEADS * RET_DV
    k_scale = RET_DK ** -0.5
    for lo in range(0, v_w, MXU_WIDTH):
        g = _dot(xg, w_ref[:, 2 * qk_w + v_w + lo:2 * qk_w + v_w + lo + MXU_WIDTH]) * inv_wide
        gate_ref[:, lo:lo + MXU_WIDTH] = (g * _sigmoid(g) * gn_ref[:, lo:lo + MXU_WIDTH]).astype(gate_ref.dtype)
        v_ref[:, lo:lo + MXU_WIDTH] = (
            _dot(xg, w_ref[:, 2 * qk_w + lo:2 * qk_w + lo + MXU_WIDTH]) * inv_wide).astype(v_ref.dtype)
    for h in range(RET_HEADS):
        lo = h * RET_DK
        r = _dot(xg, w_ref[:, lo:lo + RET_DK])
        x1, x2 = r[:, :half], r[:, half:]
        q_ref[:, lo:lo + half] = (x1 * cos - x2 * sin).astype(q_ref.dtype)
        q_ref[:, lo + half:lo + RET_DK] = (x2 * cos + x1 * sin).astype(q_ref.dtype)
        r = _dot(xg, w_ref[:, qk_w + lo:qk_w + lo + RET_DK])
        x1, x2 = r[:, :half], r[:, half:]
        k_ref[:, lo:lo + half] = ((x1 * cos - x2 * sin) * k_scale).astype(k_ref.dtype)
        k_ref[:, lo + half:lo + RET_DK] = ((x2 * cos + x1 * sin) * k_scale).astype(k_ref.dtype)


def _ret_proj(x2d, gain, w_in, gn_gain, seq, tm):
    t, d = x2d.shape
    qk_w = RET_HEADS * RET_DK
    v_w = RET_HEADS * RET_DV
    cos, sin = _rope_angles(seq, RET_DK)
    pos_blocks = seq // tm
    row = lambda i: (i, 0)
    pos = lambda i: (i % pos_blocks, 0)
    return pl.pallas_call(
        _ret_proj_kernel,
        grid=(t // tm,),
        in_specs=[
            pl.BlockSpec((tm, d), row),
            _resident((1, d)),
            _resident(w_in.shape),
            pl.BlockSpec((tm, RET_DK // 2), pos),
            pl.BlockSpec((tm, RET_DK // 2), pos),
            _resident((1, v_w)),
        ],
        out_specs=[
            pl.BlockSpec((tm, qk_w), row),
            pl.BlockSpec((tm, qk_w), row),
            pl.BlockSpec((tm, v_w), row),
            pl.BlockSpec((tm, v_w), row),
        ],
        out_shape=[
            jax.ShapeDtypeStruct((t, qk_w), MXU_DTYPE),
            jax.ShapeDtypeStruct((t, qk_w), MXU_DTYPE),
            jax.ShapeDtypeStruct((t, v_w), MXU_DTYPE),
            jax.ShapeDtypeStruct((t, v_w), MXU_DTYPE),
        ],
        compiler_params=_params("parallel"),
        name="ret_proj",
    )(x2d, gain.reshape(1, d), w_in.astype(MXU_DTYPE), cos, sin, gn_gain.reshape(1, v_w))


def _retention_kernel(q_ref, k_ref, v_ref, gate_ref, decay_ref, qd_ref, kd_ref, cd_ref, y_ref, state_ref,
                      *, groups):
    @pl.when(pl.program_id(1) == 0)
    def _():
        state_ref[...] = jnp.zeros_like(state_ref)

    def matmuls(c, h):
        rows = slice(c * RET_SUPER, (c + 1) * RET_SUPER)
        qk_cols = slice(h * RET_DK, (h + 1) * RET_DK)
        v_cols = slice(h * RET_DV, (h + 1) * RET_DV)
        q = q_ref[0, rows, qk_cols]
        k = k_ref[0, rows, qk_cols]
        v = v_ref[0, rows, v_cols]
        state = state_ref[h]
        scores = lax.dot_general(q, k, _NT, preferred_element_type=F32) * decay_ref[h]
        lhs = jnp.concatenate([q * qd_ref[h], scores.astype(MXU_DTYPE)], axis=1)
        rhs = jnp.concatenate([state.astype(MXU_DTYPE), v], axis=0)
        out = _dot(lhs, rhs)
        state_ref[h] = state * cd_ref[h] + lax.dot_general(k * kd_ref[h], v, _TN, preferred_element_type=F32)
        return out

    def finish(c, h, out):
        rows = slice(c * RET_SUPER, (c + 1) * RET_SUPER)
        v_cols = slice(h * RET_DV, (h + 1) * RET_DV)
        gate = gate_ref[0, rows, v_cols].astype(F32)
        y_ref[0, rows, v_cols] = (gate * (out * _rms_scale(out))).astype(y_ref.dtype)

    units = [(c, h) for c in range(groups) for h in range(RET_HEADS)]
    ahead = matmuls(*units[0])
    for u, unit in enumerate(units):
        out = ahead
        if u + 1 < len(units):
            ahead = matmuls(*units[u + 1])
        finish(*unit, out)


def _retention_tables():
    log_gamma = np.log(1.0 - 2.0 ** (-5.0 - np.arange(RET_HEADS, dtype=np.float64)))
    idx = np.arange(RET_SUPER, dtype=np.float64)
    dist = np.abs(idx[:, None] - idx[None, :])
    chunk = np.arange(RET_SUPER) // CHUNK
    visible = chunk[None, :] <= chunk[:, None]
    decay = np.where(visible[None], np.exp(log_gamma[:, None, None] * dist), 0.0)
    q_decay = np.exp(log_gamma[:, None] * (idx + 1.0))
    k_decay = np.exp(log_gamma[:, None] * (RET_SUPER - 1.0 - idx))
    group_decay = np.exp(log_gamma * RET_SUPER)
    qd = np.broadcast_to(q_decay[:, :, None], (RET_HEADS, RET_SUPER, RET_DK))
    kd = np.broadcast_to(k_decay[:, :, None], (RET_HEADS, RET_SUPER, RET_DK))
    cd = np.broadcast_to(group_decay[:, None, None], (RET_HEADS, 1, RET_DV))
    decay, qd, kd, cd = (jnp.asarray(np.ascontiguousarray(a), F32) for a in (decay, qd, kd, cd))
    return decay, qd.astype(MXU_DTYPE), kd.astype(MXU_DTYPE), cd


def _retention(q, k, v, gate, block):
    b, s, _ = q.shape
    qk_w = RET_HEADS * RET_DK
    v_w = RET_HEADS * RET_DV
    decay, qd, kd, cd = _retention_tables()
    tok = lambda i, j: (i, j, 0)
    return pl.pallas_call(
        functools.partial(_retention_kernel, groups=block // RET_SUPER),
        grid=(b, s // block),
        in_specs=[
            pl.BlockSpec((1, block, qk_w), tok),
            pl.BlockSpec((1, block, qk_w), tok),
            pl.BlockSpec((1, block, v_w), tok),
            pl.BlockSpec((1, block, v_w), tok),
            _resident(decay.shape),
            _resident(qd.shape),
            _resident(kd.shape),
            _resident(cd.shape),
        ],
        out_specs=pl.BlockSpec((1, block, v_w), tok),
        out_shape=jax.ShapeDtypeStruct((b, s, v_w), MXU_DTYPE),
        scratch_shapes=[pltpu.VMEM((RET_HEADS, RET_DK, RET_DV), F32)],
        compiler_params=_params("parallel", "arbitrary"),
        name="retention",
    )(q, k, v, gate, decay, qd, kd, cd)


def _tail_kernel(h_ref, y_ref, p_ref, wo_ref, mg_ref, w1_ref, w2_ref, pg_ref, wg_ref, wp_ref, o_ref,
                 *, ff_chunk):
    h = h_ref[...] + _dot(y_ref[...], wo_ref[...])
    d = h.shape[-1]
    inv = _lanes(_rms_scale_lanes(h), ff_chunk)
    hg = (h * mg_ref[...]).astype(MXU_DTYPE)
    d_ff = w1_ref.shape[1]
    mlp = None
    for c in range(d_ff // ff_chunk):
        cols = slice(c * ff_chunk, (c + 1) * ff_chunk)
        a = jnp.square(jnp.maximum(_dot(hg, w1_ref[:, cols]) * inv, 0.0))
        part = _dot(a.astype(MXU_DTYPE), w2_ref[cols, :])
        mlp = part if mlp is None else mlp + part
    h = h + mlp
    inv = _rms_scale_lanes(h)
    hg = (h * pg_ref[...]).astype(MXU_DTYPE)
    gate = _sigmoid(_dot(hg, wg_ref[...]) * _lanes(inv, d))
    emb = _dot(p_ref[...].astype(MXU_DTYPE), wp_ref[...])
    o_ref[...] = h + gate * emb


def _layer_of(shape, layer):
    index = (layer,) + (0,) * (len(shape) - 1)
    return pl.BlockSpec((None,) + tuple(shape[1:]), lambda *_: index, pipeline_mode=pl.Buffered(1))


def _tail(h2d, y2d, p3d, w_out, mlp_gain, w1, w2, ple_gain, wg, wp, layer, tm, ff_chunk):
    t, d = h2d.shape
    row = lambda i: (i, 0)
    return pl.pallas_call(
        functools.partial(_tail_kernel, ff_chunk=ff_chunk),
        grid=(t // tm,),
        in_specs=[
            pl.BlockSpec((tm, d), row),
            pl.BlockSpec((tm, y2d.shape[1]), row),
            pl.BlockSpec((None, tm, p3d.shape[2]), lambda i: (layer, i, 0)),
            _resident(w_out.shape),
            _layer_of(mlp_gain.shape, layer),
            _layer_of(w1.shape, layer),
            _layer_of(w2.shape, layer),
            _layer_of(ple_gain.shape, layer),
            _layer_of(wg.shape, layer),
            _layer_of(wp.shape, layer),
        ],
        out_specs=pl.BlockSpec((tm, d), row),
        out_shape=jax.ShapeDtypeStruct((t, d), F32),
        compiler_params=_params("parallel"),
        name="layer_tail",
    )(h2d, y2d, p3d, w_out.astype(MXU_DTYPE), mlp_gain, w1, w2, ple_gain, wg, wp)


def _mla_proj_kernel(x_ref, g_ref, win_ref, qa_ref, kva_ref, wuq_ref, wukv_ref, qn_ref, kn_ref,
                     qrope_ref, krope_ref, q_ref, k_ref, v_ref, *, sub_tiles):
    tm = x_ref.shape[0] // sub_tiles
    d = x_ref.shape[-1]
    lane = lax.broadcasted_iota(jnp.int32, (1, LANES), 1)
    first_half = (lane < MLA_ROPE).astype(F32)
    softmax_scale = MLA_QKD ** -0.5 * LOG2_E
    qn = qn_ref[...]
    kn = kn_ref[...]
    ones = jnp.ones((MLA_NOPE, LANES), MXU_DTYPE)
    halves = jnp.full((LANES, LANES), 0.5, MXU_DTYPE)
    head_weights = jnp.concatenate([ones, halves], axis=0)

    def down(rows):
        x = x_ref[rows, :]
        inv = lax.rsqrt(_row_sumsq(x) * (1.0 / d) + EPS)
        proj = _dot((x * g_ref[...]).astype(MXU_DTYPE), win_ref[...])
        proj = proj * _lanes(inv, proj.shape[-1])
        c_q = proj[:, :MLA_Q_RANK]
        c_kv = proj[:, MLA_Q_RANK:MLA_Q_RANK + MLA_KV_RANK]
        k_r = proj[:, MLA_Q_RANK + MLA_KV_RANK:]
        inv = lax.rsqrt(_row_sumsq(c_q) * (1.0 / MLA_Q_RANK) + EPS)
        c_q = (c_q * _lanes(inv, MLA_Q_RANK) * qa_ref[...]).astype(MXU_DTYPE)
        inv = lax.rsqrt(_row_sumsq(c_kv) * (1.0 / MLA_KV_RANK) + EPS)
        c_kv = (c_kv * _lanes(inv, MLA_KV_RANK) * kva_ref[...]).astype(MXU_DTYPE)
        k_rot = k_r * krope_ref[rows, :]
        k_rot = (k_rot + pltpu.roll(k_rot, MLA_ROPE, 1)) * first_half
        return c_q, c_kv, k_rot, _row_sumsq(k_r, halves)

    tiles = [slice(i * tm, (i + 1) * tm) for i in range(sub_tiles)]
    projected = [down(rows) for rows in tiles]

    def up_project(t, h):
        c_q, c_kv = projected[t][:2]
        cols = slice(h * MLA_HEAD_W, (h + 1) * MLA_HEAD_W)
        return _dot(c_q, wuq_ref[:, cols]), _dot(c_kv, wukv_ref[:, cols])

    def normalise(t, h, qh, kvh):
        rows = tiles[t]
        k_rot, k_r_ss = projected[t][2:]
        lo = h * MLA_HEAD_W
        q_nope, q_r = qh[:, :MLA_NOPE], qh[:, MLA_NOPE:]
        inv = lax.rsqrt(_row_sumsq(qh, head_weights) * (1.0 / MLA_QKD) + EPS) * softmax_scale
        q_rot = q_r * qrope_ref[rows, :]
        q_rot = q_rot + pltpu.roll(q_rot, MLA_ROPE, 1)
        q_ref[rows, lo:lo + MLA_NOPE] = (q_nope * inv * qn).astype(q_ref.dtype)
        q_ref[rows, lo + MLA_NOPE:lo + MLA_HEAD_W] = (q_rot * inv).astype(q_ref.dtype)

        k_nope, v = kvh[:, :MLA_NOPE], kvh[:, MLA_NOPE:]
        inv = lax.rsqrt((_row_sumsq(k_nope, ones) + k_r_ss) * (1.0 / MLA_QKD) + EPS)
        k_ref[rows, lo:lo + MLA_NOPE] = (k_nope * inv * kn).astype(k_ref.dtype)
        k_ref[rows, lo + MLA_NOPE:lo + MLA_HEAD_W] = (k_rot * inv).astype(k_ref.dtype)
        v_ref[rows, h * MLA_VD:(h + 1) * MLA_VD] = v.astype(v_ref.dtype)

    units = [(t, h) for t in range(sub_tiles) for h in range(MLA_HEADS)]
    ahead = up_project(*units[0])
    for u, unit in enumerate(units):
        current = ahead
        if u + 1 < len(units):
            ahead = up_project(*units[u + 1])
        normalise(*unit, *current)


def _rotate_half_cols(w):
    half = w.shape[-1] // 2
    return jnp.concatenate([-w[..., half:], w[..., :half]], axis=-1)


def _rope_gain_table(seq, gain_rope):
    cos, sin = _rope_angles(seq, MLA_ROPE)
    base = jnp.concatenate([cos, cos, sin, sin], axis=-1)
    half = MLA_ROPE // 2
    gains = jnp.concatenate([gain_rope, gain_rope[half:], gain_rope[:half]])
    return base * gains[None, :]


def _mla_proj(x2d, gain, w_in, q_a_gain, kv_a_gain, w_uq, w_ukv, q_gain, k_gain, seq, tm, sub_tiles):
    t, d = x2d.shape
    kr_lo = MLA_Q_RANK + MLA_KV_RANK
    w_in_ext = jnp.concatenate([w_in, _rotate_half_cols(w_in[:, kr_lo:])], axis=1)
    w_uq_h = w_uq.reshape(MLA_Q_RANK, MLA_HEADS, MLA_QKD)
    w_uq_ext = jnp.concatenate([w_uq_h, _rotate_half_cols(w_uq_h[..., MLA_NOPE:])], axis=-1)
    w_uq_ext = w_uq_ext.reshape(MLA_Q_RANK, MLA_HEADS * MLA_HEAD_W)
    q_rope = _rope_gain_table(seq, q_gain[MLA_NOPE:])
    k_rope = _rope_gain_table(seq, k_gain[MLA_NOPE:])
    qk_w = MLA_HEADS * MLA_HEAD_W
    v_w = MLA_HEADS * MLA_VD
    block = sub_tiles * tm
    pos_blocks = seq // block
    row = lambda i: (i, 0)
    pos = lambda i: (i % pos_blocks, 0)
    return pl.pallas_call(
        functools.partial(_mla_proj_kernel, sub_tiles=sub_tiles),
        grid=(t // block,),
        in_specs=[
            pl.BlockSpec((block, d), row),
            _resident((1, d)),
            _resident(w_in_ext.shape),
            _resident((1, MLA_Q_RANK)),
            _resident((1, MLA_KV_RANK)),
            _resident(w_uq_ext.shape),
            _resident(w_ukv.shape),
            _resident((1, MLA_NOPE)),
            _resident((1, MLA_NOPE)),
            pl.BlockSpec((block, LANES), pos),
            pl.BlockSpec((block, LANES), pos),
        ],
        out_specs=[
            pl.BlockSpec((block, qk_w), row),
            pl.BlockSpec((block, qk_w), row),
            pl.BlockSpec((block, v_w), row),
        ],
        out_shape=[
            jax.ShapeDtypeStruct((t, qk_w), MXU_DTYPE),
            jax.ShapeDtypeStruct((t, qk_w), MXU_DTYPE),
            jax.ShapeDtypeStruct((t, v_w), MXU_DTYPE),
        ],
        compiler_params=_params("parallel"),
        name="mla_proj",
    )(x2d, gain.reshape(1, d), w_in_ext.astype(MXU_DTYPE), q_a_gain.reshape(1, -1), kv_a_gain.reshape(1, -1),
      w_uq_ext.astype(MXU_DTYPE), w_ukv.astype(MXU_DTYPE), q_gain[:MLA_NOPE].reshape(1, -1),
      k_gain[:MLA_NOPE].reshape(1, -1), q_rope, k_rope)


def _attn_kernel(q_ref, k_ref, v_ref, o_ref, m_ref, acc_ref, *, block):
    n_blocks = q_ref.shape[1] // block
    half = block // 2

    units = []
    for qb in range(n_blocks):
        q0 = qb * block
        units += [((q0, block), (kb * block, block)) for kb in range(qb)]
        units += [((q0, half), (q0, half)), ((q0 + half, half), (q0, block))]

    def span(start_size):
        return slice(start_size[0], start_size[0] + start_size[1])

    def scores(q_rows, k_rows):
        return lax.dot_general(q_ref[0, span(q_rows), :], k_ref[0, span(k_rows), :], _NT,
                               preferred_element_type=F32)

    def update(s, q_rows, k_rows):
        (q0, nq), (k0, nk) = q_rows, k_rows
        slot = (q0 // block) % 2
        local = slice(q0 % block, q0 % block + nq)
        first = k0 == 0
        last = k0 + nk == q0 + nq
        if last:
            q_chunk = (q0 + lax.broadcasted_iota(jnp.int32, s.shape, 0)) // CHUNK
            k_chunk = (k0 + lax.broadcasted_iota(jnp.int32, s.shape, 1)) // CHUNK
            s = jnp.where(k_chunk <= q_chunk, s, -1e30)
        m_new = jnp.broadcast_to(jnp.max(s, axis=-1, keepdims=True), (nq, LANES))
        if not first:
            m_prev = m_ref[slot, local, :]
            m_new = jnp.maximum(m_prev, m_new)
        p = jnp.exp2(s - jnp.tile(m_new, (1, nk // LANES)))
        ones = jnp.ones((nk, LANES), MXU_DTYPE)
        acc = _dot(p.astype(MXU_DTYPE), jnp.concatenate([v_ref[0, span(k_rows), :], ones], axis=1))
        if not first:
            acc = jnp.tile(jnp.exp2(m_prev - m_new), (1, 2)) * acc_ref[slot, local, :] + acc
        if last:
            o_ref[0, span(q_rows), :] = (acc[:, :MLA_VD] / acc[:, MLA_VD:]).astype(o_ref.dtype)
        else:
            m_ref[slot, local, :] = m_new
            acc_ref[slot, local, :] = acc

    s_next = scores(*units[0])
    for u, unit in enumerate(units):
        s = s_next
        if u + 1 < len(units):
            s_next = scores(*units[u + 1])
        update(s, *unit)


def _attention(q, k, v, block):
    b, s, _ = q.shape
    return pl.pallas_call(
        functools.partial(_attn_kernel, block=block),
        grid=(b, MLA_HEADS),
        in_specs=[
            pl.BlockSpec((1, s, MLA_HEAD_W), lambda i, h: (i, 0, h)),
            pl.BlockSpec((1, s, MLA_HEAD_W), lambda i, h: (i, 0, h)),
            pl.BlockSpec((1, s, MLA_VD), lambda i, h: (i, 0, h)),
        ],
        out_specs=pl.BlockSpec((1, s, MLA_VD), lambda i, h: (i, 0, h)),
        out_shape=jax.ShapeDtypeStruct((b, s, MLA_HEADS * MLA_VD), MXU_DTYPE),
        scratch_shapes=[
            pltpu.VMEM((2, block, LANES), F32),
            pltpu.VMEM((2, block, MLA_VD + LANES), F32),
        ],
        compiler_params=_params("parallel", "parallel"),
        name="mla_attention",
    )(q, k, v)


def _tiles(seq):
    assert seq % RET_SUPER == 0, seq
    tm = 512 if seq % 512 == 0 else RET_SUPER
    mla_sub_tiles = 2 if seq % (2 * tm) == 0 else 1
    ret_block = 2 * tm if seq % (2 * tm) == 0 else tm
    return dict(tm=tm, ret_block=ret_block, attn_block=tm, ff_chunk=512, mla_sub_tiles=mla_sub_tiles)


def kernel(x, p, mix_norm, ret_w_in, ret_gn, ret_w_out, mla_w_in, mla_q_a_norm, mla_kv_a_norm, mla_w_uq,
           mla_w_ukv, mla_q_norm, mla_k_norm, mla_w_out, mlp_norm, mlp_w1, mlp_w2, ple_norm, ple_gate_w,
           ple_proj_w):
    b, s, d = x.shape
    depth = p.shape[0]
    t = b * s
    cfg = _tiles(s)
    tm = cfg["tm"]
    h = x.reshape(t, d)
    p3d = p.reshape(depth, t, -1)
    mlp_gain = mlp_norm.reshape(depth, 1, d)
    ple_gain = ple_norm.reshape(depth, 1, d)
    w1, w2, wg, wp = (w.astype(MXU_DTYPE) for w in (mlp_w1, mlp_w2, ple_gate_w, ple_proj_w))
    for i in range(depth):
        j = i // 2
        if i % 2 == 0:
            q, k, v, gate = _ret_proj(h, mix_norm[i], ret_w_in[j], ret_gn[j], s, tm)
            y = _retention(q.reshape(b, s, -1), k.reshape(b, s, -1), v.reshape(b, s, -1),
                           gate.reshape(b, s, -1), cfg["ret_block"])
            w_out = ret_w_out[j]
        else:
            q, k, v = _mla_proj(h, mix_norm[i], mla_w_in[j], mla_q_a_norm[j], mla_kv_a_norm[j], mla_w_uq[j],
                                mla_w_ukv[j], mla_q_norm[j], mla_k_norm[j], s, tm, cfg["mla_sub_tiles"])
            y = _attention(q.reshape(b, s, -1), k.reshape(b, s, -1), v.reshape(b, s, -1), cfg["attn_block"])
            w_out = mla_w_out[j]
        h = _tail(h, y.reshape(t, -1), p3d, w_out, mlp_gain, w1, w2, ple_gain, wg, wp, i, tm, cfg["ff_chunk"])
    return h.reshape(b, s, d)
```

```python
import functools

import jax
import jax.numpy as jnp
import numpy as np
from jax import lax
from jax.experimental import pallas as pl
from jax.experimental.pallas import tpu as pltpu

EPS = 1e-6
ROPE_THETA = 10000.0
CHUNK = 64
LANES = 128
MXU_WIDTH = 256

RET_HEADS = 4
RET_DK = 256
RET_DV = 512
RET_SUPER = 256

MLA_HEADS = 8
MLA_NOPE = 128
MLA_ROPE = 64
MLA_QKD = MLA_NOPE + MLA_ROPE
MLA_VD = 128
MLA_Q_RANK = 384
MLA_KV_RANK = 256
MLA_HEAD_W = 256

MXU_DTYPE = jnp.bfloat16
F32 = jnp.float32
LOG2_E = 1.4426950408889634
VMEM_LIMIT = 56 * 1024 * 1024

_NT = (((1,), (1,)), ((), ()))
_TN = (((0,), (0,)), ((), ()))


def _dot(a, b):
    return jnp.dot(a, b, preferred_element_type=F32)


def _rms_scale(x):
    return lax.rsqrt(jnp.mean(x * x, axis=-1, keepdims=True) + EPS)


def _rms_scale_lanes(x):
    return jnp.broadcast_to(_rms_scale(x), (x.shape[0], LANES))


def _row_sumsq(x, weights=None):
    if weights is None:
        weights = jnp.ones((x.shape[-1], LANES), MXU_DTYPE)
    return _dot((x * x).astype(MXU_DTYPE), weights)


def _lanes(a, width):
    return a if width == LANES else jnp.tile(a, (1, width // LANES))


def _sigmoid(x):
    return 0.5 * jnp.tanh(0.5 * x) + 0.5


def _resident(shape):
    zeros = (0,) * len(shape)
    return pl.BlockSpec(shape, lambda *_: zeros, pipeline_mode=pl.Buffered(1))


def _params(*semantics):
    return pltpu.CompilerParams(dimension_semantics=semantics, vmem_limit_bytes=VMEM_LIMIT)


def _rope_angles(seq, dim):
    inv = 1.0 / (ROPE_THETA ** (jnp.arange(0, dim, 2, dtype=F32) / dim))
    ang = jnp.arange(seq, dtype=F32)[:, None] * inv[None, :]
    return jnp.cos(ang), jnp.sin(ang)


def _ret_proj_kernel(x_ref, g_ref, w_ref, cos_ref, sin_ref, gn_ref, q_ref, k_ref, v_ref, gate_ref):
    x = x_ref[...]
    inv = _rms_scale_lanes(x)
    xg = (x * g_ref[...]).astype(MXU_DTYPE)
    inv_wide = _lanes(inv, MXU_WIDTH)
    cos = cos_ref[...] * inv
    sin = sin_ref[...] * inv
    half = RET_DK // 2
    qk_w = RET_HEADS * RET_DK
    v_w = RET_HEADS * RET_DV
    k_scale = RET_DK ** -0.5
    for lo in range(0, v_w, MXU_WIDTH):
        g = _dot(xg, w_ref[:, 2 * qk_w + v_w + lo:2 * qk_w + v_w + lo + MXU_WIDTH]) * inv_wide
        gate_ref[:, lo:lo + MXU_WIDTH] = (g * _sigmoid(g) * gn_ref[:, lo:lo + MXU_WIDTH]).astype(gate_ref.dtype)
        v_ref[:, lo:lo + MXU_WIDTH] = (
            _dot(xg, w_ref[:, 2 * qk_w + lo:2 * qk_w + lo + MXU_WIDTH]) * inv_wide).astype(v_ref.dtype)
    for h in range(RET_HEADS):
        lo = h * RET_DK
        r = _dot(xg, w_ref[:, lo:lo + RET_DK])
        x1, x2 = r[:, :half], r[:, half:]
        q_ref[:, lo:lo + half] = (x1 * cos - x2 * sin).astype(q_ref.dtype)
        q_ref[:, lo + half:lo + RET_DK] = (x2 * cos + x1 * sin).astype(q_ref.dtype)
        r = _dot(xg, w_ref[:, qk_w + lo:qk_w + lo + RET_DK])
        x1, x2 = r[:, :half], r[:, half:]
        k_ref[:, lo:lo + half] = ((x1 * cos - x2 * sin) * k_scale).astype(k_ref.dtype)
        k_ref[:, lo + half:lo + RET_DK] = ((x2 * cos + x1 * sin) * k_scale).astype(k_ref.dtype)


def _ret_proj(x2d, gain, w_in, gn_gain, seq, tm):
    t, d = x2d.shape
    qk_w = RET_HEADS * RET_DK
    v_w = RET_HEADS * RET_DV
    cos, sin = _rope_angles(seq, RET_DK)
    pos_blocks = seq // tm
    row = lambda i: (i, 0)
    pos = lambda i: (i % pos_blocks, 0)
    return pl.pallas_call(
        _ret_proj_kernel,
        grid=(t // tm,),
        in_specs=[
            pl.BlockSpec((tm, d), row),
            _resident((1, d)),
            _resident(w_in.shape),
            pl.BlockSpec((tm, RET_DK // 2), pos),
            pl.BlockSpec((tm, RET_DK // 2), pos),
            _resident((1, v_w)),
        ],
        out_specs=[
            pl.BlockSpec((tm, qk_w), row),
            pl.BlockSpec((tm, qk_w), row),
            pl.BlockSpec((tm, v_w), row),
            pl.BlockSpec((tm, v_w), row),
        ],
        out_shape=[
            jax.ShapeDtypeStruct((t, qk_w), MXU_DTYPE),
            jax.ShapeDtypeStruct((t, qk_w), MXU_DTYPE),
            jax.ShapeDtypeStruct((t, v_w), MXU_DTYPE),
            jax.ShapeDtypeStruct((t, v_w), MXU_DTYPE),
        ],
        compiler_params=_params("parallel"),
        name="ret_proj",
    )(x2d, gain.reshape(1, d), w_in.astype(MXU_DTYPE), cos, sin, gn_gain.reshape(1, v_w))


def _retention_kernel(q_ref, k_ref, v_ref, gate_ref, decay_ref, qd_ref, kd_ref, cd_ref, y_ref, state_ref,
                      *, groups):
    @pl.when(pl.program_id(1) == 0)
    def _():
        state_ref[...] = jnp.zeros_like(state_ref)

    def matmuls(c, h):
        rows = slice(c * RET_SUPER, (c + 1) * RET_SUPER)
        qk_cols = slice(h * RET_DK, (h + 1) * RET_DK)
        v_cols = slice(h * RET_DV, (h + 1) * RET_DV)
        q = q_ref[0, rows, qk_cols]
        k = k_ref[0, rows, qk_cols]
        v = v_ref[0, rows, v_cols]
        state = state_ref[h]
        scores = lax.dot_general(q, k, _NT, preferred_element_type=F32) * decay_ref[h]
        lhs = jnp.concatenate([q * qd_ref[h], scores.astype(MXU_DTYPE)], axis=1)
        rhs = jnp.concatenate([state.astype(MXU_DTYPE), v], axis=0)
        out = _dot(lhs, rhs)
        state_ref[h] = state * cd_ref[h] + lax.dot_general(k * kd_ref[h], v, _TN, preferred_element_type=F32)
        return out

    def finish(c, h, out):
        rows = slice(c * RET_SUPER, (c + 1) * RET_SUPER)
        v_cols = slice(h * RET_DV, (h + 1) * RET_DV)
        gate = gate_ref[0, rows, v_cols]
        y_ref[0, rows, v_cols] = gate * (out * _rms_scale(out)).astype(y_ref.dtype)

    units = [(c, h) for c in range(groups) for h in range(RET_HEADS)]
    ahead = matmuls(*units[0])
    for u, unit in enumerate(units):
        out = ahead
        if u + 1 < len(units):
            ahead = matmuls(*units[u + 1])
        finish(*unit, out)


def _retention_tables():
    log_gamma = np.log(1.0 - 2.0 ** (-5.0 - np.arange(RET_HEADS, dtype=np.float64)))
    idx = np.arange(RET_SUPER, dtype=np.float64)
    dist = np.abs(idx[:, None] - idx[None, :])
    chunk = np.arange(RET_SUPER) // CHUNK
    visible = chunk[None, :] <= chunk[:, None]
    decay = np.where(visible[None], np.exp(log_gamma[:, None, None] * dist), 0.0)
    q_decay = np.exp(log_gamma[:, None] * (idx + 1.0))
    k_decay = np.exp(log_gamma[:, None] * (RET_SUPER - 1.0 - idx))
    group_decay = np.exp(log_gamma * RET_SUPER)
    qd = np.broadcast_to(q_decay[:, :, None], (RET_HEADS, RET_SUPER, RET_DK))
    kd = np.broadcast_to(k_decay[:, :, None], (RET_HEADS, RET_SUPER, RET_DK))
    cd = np.broadcast_to(group_decay[:, None, None], (RET_HEADS, 1, RET_DV))
    decay, qd, kd, cd = (jnp.asarray(np.ascontiguousarray(a), F32) for a in (decay, qd, kd, cd))
    return decay, qd.astype(MXU_DTYPE), kd.astype(MXU_DTYPE), cd


def _retention(q, k, v, gate, block):
    b, s, _ = q.shape
    qk_w = RET_HEADS * RET_DK
    v_w = RET_HEADS * RET_DV
    decay, qd, kd, cd = _retention_tables()
    tok = lambda i, j: (i, j, 0)
    return pl.pallas_call(
        functools.partial(_retention_kernel, groups=block // RET_SUPER),
        grid=(b, s // block),
        in_specs=[
            pl.BlockSpec((1, block, qk_w), tok),
            pl.BlockSpec((1, block, qk_w), tok),
            pl.BlockSpec((1, block, v_w), tok),
            pl.BlockSpec((1, block, v_w), tok),
            _resident(decay.shape),
            _resident(qd.shape),
            _resident(kd.shape),
            _resident(cd.shape),
        ],
        out_specs=pl.BlockSpec((1, block, v_w), tok),
        out_shape=jax.ShapeDtypeStruct((b, s, v_w), MXU_DTYPE),
        scratch_shapes=[pltpu.VMEM((RET_HEADS, RET_DK, RET_DV), F32)],
        compiler_params=_params("parallel", "arbitrary"),
        name="retention",
    )(q, k, v, gate, decay, qd, kd, cd)


def _tail_kernel(h_ref, y_ref, p_ref, wo_ref, mg_ref, w1_ref, w2_ref, pg_ref, wg_ref, wp_ref, o_ref,
                 *, ff_chunk):
    h = h_ref[...] + _dot(y_ref[...], wo_ref[...])
    d = h.shape[-1]
    inv = _lanes(_rms_scale_lanes(h), ff_chunk)
    hg = (h * mg_ref[...]).astype(MXU_DTYPE)
    d_ff = w1_ref.shape[1]
    mlp = None
    for c in range(d_ff // ff_chunk):
        cols = slice(c * ff_chunk, (c + 1) * ff_chunk)
        a = jnp.square(jnp.maximum(_dot(hg, w1_ref[:, cols]) * inv, 0.0))
        part = _dot(a.astype(MXU_DTYPE), w2_ref[cols, :])
        mlp = part if mlp is None else mlp + part
    h = h + mlp
    inv = _lanes(_rms_scale_lanes(h), MXU_WIDTH)
    hg = (h * pg_ref[...]).astype(MXU_DTYPE)
    pe = p_ref[...].astype(MXU_DTYPE)
    for lo in range(0, d, MXU_WIDTH):
        cols = slice(lo, lo + MXU_WIDTH)
        gate = _sigmoid(_dot(hg, wg_ref[:, cols]) * inv)
        o_ref[:, cols] = h[:, cols] + gate * _dot(pe, wp_ref[:, cols])


def _layer_of(shape, layer):
    index = (layer,) + (0,) * (len(shape) - 1)
    return pl.BlockSpec((None,) + tuple(shape[1:]), lambda *_: index, pipeline_mode=pl.Buffered(1))


def _tail(h2d, y2d, p3d, w_out, mlp_gain, w1, w2, ple_gain, wg, wp, layer, tm, ff_chunk):
    t, d = h2d.shape
    row = lambda i: (i, 0)
    return pl.pallas_call(
        functools.partial(_tail_kernel, ff_chunk=ff_chunk),
        grid=(t // tm,),
        in_specs=[
            pl.BlockSpec((tm, d), row),
            pl.BlockSpec((tm, y2d.shape[1]), row),
            pl.BlockSpec((None, tm, p3d.shape[2]), lambda i: (layer, i, 0)),
            _resident(w_out.shape),
            _layer_of(mlp_gain.shape, layer),
            _layer_of(w1.shape, layer),
            _layer_of(w2.shape, layer),
            _layer_of(ple_gain.shape, layer),
            _layer_of(wg.shape, layer),
            _layer_of(wp.shape, layer),
        ],
        out_specs=pl.BlockSpec((tm, d), row),
        out_shape=jax.ShapeDtypeStruct((t, d), F32),
        compiler_params=_params("parallel"),
        name="layer_tail",
    )(h2d, y2d, p3d, w_out.astype(MXU_DTYPE), mlp_gain, w1, w2, ple_gain, wg, wp)


def _mla_proj_kernel(x_ref, g_ref, win_ref, qa_ref, kva_ref, wuq_ref, wukv_ref, qn_ref, kn_ref,
                     qrope_ref, krope_ref, q_ref, k_ref, v_ref, *, sub_tiles):
    tm = x_ref.shape[0] // sub_tiles
    d = x_ref.shape[-1]
    lane = lax.broadcasted_iota(jnp.int32, (1, LANES), 1)
    first_half = (lane < MLA_ROPE).astype(F32)
    softmax_scale = MLA_QKD ** -0.5 * LOG2_E
    qn = qn_ref[...]
    kn = kn_ref[...]
    ones = jnp.ones((MLA_NOPE, LANES), MXU_DTYPE)
    halves = jnp.full((LANES, LANES), 0.5, MXU_DTYPE)
    head_weights = jnp.concatenate([ones, halves], axis=0)

    def down(rows):
        x = x_ref[rows, :]
        inv = lax.rsqrt(_row_sumsq(x) * (1.0 / d) + EPS)
        proj = _dot((x * g_ref[...]).astype(MXU_DTYPE), win_ref[...])
        proj = proj * _lanes(inv, proj.shape[-1])
        c_q = proj[:, :MLA_Q_RANK]
        c_kv = proj[:, MLA_Q_RANK:MLA_Q_RANK + MLA_KV_RANK]
        k_r = proj[:, MLA_Q_RANK + MLA_KV_RANK:]
        inv = lax.rsqrt(_row_sumsq(c_q) * (1.0 / MLA_Q_RANK) + EPS)
        c_q = (c_q * _lanes(inv, MLA_Q_RANK) * qa_ref[...]).astype(MXU_DTYPE)
        inv = lax.rsqrt(_row_sumsq(c_kv) * (1.0 / MLA_KV_RANK) + EPS)
        c_kv = (c_kv * _lanes(inv, MLA_KV_RANK) * kva_ref[...]).astype(MXU_DTYPE)
        k_rot = k_r * krope_ref[rows, :]
        k_rot = (k_rot + pltpu.roll(k_rot, MLA_ROPE, 1)) * first_half
        return c_q, c_kv, k_rot, _row_sumsq(k_r, halves)

    tiles = [slice(i * tm, (i + 1) * tm) for i in range(sub_tiles)]
    projected = [down(rows) for rows in tiles]

    def up_project(t, h):
        c_q, c_kv = projected[t][:2]
        cols = slice(h * MLA_HEAD_W, (h + 1) * MLA_HEAD_W)
        return _dot(c_q, wuq_ref[:, cols]), _dot(c_kv, wukv_ref[:, cols])

    def normalise(t, h, qh, kvh):
        rows = tiles[t]
        k_rot, k_r_ss = projected[t][2:]
        lo = h * MLA_HEAD_W
        q_nope, q_r = qh[:, :MLA_NOPE], qh[:, MLA_NOPE:]
        inv = lax.rsqrt(_row_sumsq(qh, head_weights) * (1.0 / MLA_QKD) + EPS) * softmax_scale
        q_rot = q_r * qrope_ref[rows, :]
        q_rot = q_rot + pltpu.roll(q_rot, MLA_ROPE, 1)
        q_ref[rows, lo:lo + MLA_NOPE] = (q_nope * inv * qn).astype(q_ref.dtype)
        q_ref[rows, lo + MLA_NOPE:lo + MLA_HEAD_W] = (q_rot * inv).astype(q_ref.dtype)

        k_nope, v = kvh[:, :MLA_NOPE], kvh[:, MLA_NOPE:]
        inv = lax.rsqrt((_row_sumsq(k_nope, ones) + k_r_ss) * (1.0 / MLA_QKD) + EPS)
        k_ref[rows, lo:lo + MLA_NOPE] = (k_nope * inv * kn).astype(k_ref.dtype)
        k_ref[rows, lo + MLA_NOPE:lo + MLA_HEAD_W] = (k_rot * inv).astype(k_ref.dtype)
        v_ref[rows, h * MLA_VD:(h + 1) * MLA_VD] = v.astype(v_ref.dtype)

    units = [(t, h) for t in range(sub_tiles) for h in range(MLA_HEADS)]
    ahead = up_project(*units[0])
    for u, unit in enumerate(units):
        current = ahead
        if u + 1 < len(units):
            ahead = up_project(*units[u + 1])
        normalise(*unit, *current)


def _rotate_half_cols(w):
    half = w.shape[-1] // 2
    return jnp.concatenate([-w[..., half:], w[..., :half]], axis=-1)


def _rope_gain_table(seq, gain_rope):
    cos, sin = _rope_angles(seq, MLA_ROPE)
    base = jnp.concatenate([cos, cos, sin, sin], axis=-1)
    half = MLA_ROPE // 2
    gains = jnp.concatenate([gain_rope, gain_rope[half:], gain_rope[:half]])
    return base * gains[None, :]


def _mla_proj(x2d, gain, w_in, q_a_gain, kv_a_gain, w_uq, w_ukv, q_gain, k_gain, seq, tm, sub_tiles):
    t, d = x2d.shape
    kr_lo = MLA_Q_RANK + MLA_KV_RANK
    w_in_ext = jnp.concatenate([w_in, _rotate_half_cols(w_in[:, kr_lo:])], axis=1)
    w_uq_h = w_uq.reshape(MLA_Q_RANK, MLA_HEADS, MLA_QKD)
    w_uq_ext = jnp.concatenate([w_uq_h, _rotate_half_cols(w_uq_h[..., MLA_NOPE:])], axis=-1)
    w_uq_ext = w_uq_ext.reshape(MLA_Q_RANK, MLA_HEADS * MLA_HEAD_W)
    q_rope = _rope_gain_table(seq, q_gain[MLA_NOPE:])
    k_rope = _rope_gain_table(seq, k_gain[MLA_NOPE:])
    qk_w = MLA_HEADS * MLA_HEAD_W
    v_w = MLA_HEADS * MLA_VD
    block = sub_tiles * tm
    pos_blocks = seq // block
    row = lambda i: (i, 0)
    pos = lambda i: (i % pos_blocks, 0)
    return pl.pallas_call(
        functools.partial(_mla_proj_kernel, sub_tiles=sub_tiles),
        grid=(t // block,),
        in_specs=[
            pl.BlockSpec((block, d), row),
            _resident((1, d)),
            _resident(w_in_ext.shape),
            _resident((1, MLA_Q_RANK)),
            _resident((1, MLA_KV_RANK)),
            _resident(w_uq_ext.shape),
            _resident(w_ukv.shape),
            _resident((1, MLA_NOPE)),
            _resident((1, MLA_NOPE)),
            pl.BlockSpec((block, LANES), pos),
            pl.BlockSpec((block, LANES), pos),
        ],
        out_specs=[
            pl.BlockSpec((block, qk_w), row),
            pl.BlockSpec((block, qk_w), row),
            pl.BlockSpec((block, v_w), row),
        ],
        out_shape=[
            jax.ShapeDtypeStruct((t, qk_w), MXU_DTYPE),
            jax.ShapeDtypeStruct((t, qk_w), MXU_DTYPE),
            jax.ShapeDtypeStruct((t, v_w), MXU_DTYPE),
        ],
        compiler_params=_params("parallel"),
        name="mla_proj",
    )(x2d, gain.reshape(1, d), w_in_ext.astype(MXU_DTYPE), q_a_gain.reshape(1, -1), kv_a_gain.reshape(1, -1),
      w_uq_ext.astype(MXU_DTYPE), w_ukv.astype(MXU_DTYPE), q_gain[:MLA_NOPE].reshape(1, -1),
      k_gain[:MLA_NOPE].reshape(1, -1), q_rope, k_rope)


def _attn_kernel(q_ref, k_ref, v_ref, o_ref, m_ref, acc_ref, *, block):
    n_blocks = q_ref.shape[1] // block
    half = block // 2

    units = []
    for qb in range(n_blocks):
        q0 = qb * block
        units += [((q0, block), (kb * block, block)) for kb in range(qb)]
        units += [((q0, half), (q0, half)), ((q0 + half, half), (q0, block))]

    def span(start_size):
        return slice(start_size[0], start_size[0] + start_size[1])

    def scores(q_rows, k_rows):
        return lax.dot_general(q_ref[0, span(q_rows), :], k_ref[0, span(k_rows), :], _NT,
                               preferred_element_type=F32)

    def update(s, q_rows, k_rows):
        (q0, nq), (k0, nk) = q_rows, k_rows
        slot = (q0 // block) % 2
        local = slice(q0 % block, q0 % block + nq)
        first = k0 == 0
        last = k0 + nk == q0 + nq
        if last:
            q_chunk = (q0 + lax.broadcasted_iota(jnp.int32, s.shape, 0)) // CHUNK
            k_chunk = (k0 + lax.broadcasted_iota(jnp.int32, s.shape, 1)) // CHUNK
            s = jnp.where(k_chunk <= q_chunk, s, -1e30)
        m_new = jnp.broadcast_to(jnp.max(s, axis=-1, keepdims=True), (nq, LANES))
        if not first:
            m_prev = m_ref[slot, local, :]
            m_new = jnp.maximum(m_prev, m_new)
        p = jnp.exp2(s - jnp.tile(m_new, (1, nk // LANES)))
        ones = jnp.ones((nk, LANES), MXU_DTYPE)
        acc = _dot(p.astype(MXU_DTYPE), jnp.concatenate([v_ref[0, span(k_rows), :], ones], axis=1))
        if not first:
            acc = jnp.tile(jnp.exp2(m_prev - m_new), (1, 2)) * acc_ref[slot, local, :] + acc
        if last:
            o_ref[0, span(q_rows), :] = (acc[:, :MLA_VD] / acc[:, MLA_VD:]).astype(o_ref.dtype)
        else:
            m_ref[slot, local, :] = m_new
            acc_ref[slot, local, :] = acc

    s_next = scores(*units[0])
    for u, unit in enumerate(units):
        s = s_next
        if u + 1 < len(units):
            s_next = scores(*units[u + 1])
        update(s, *unit)


def _attention(q, k, v, block):
    b, s, _ = q.shape
    return pl.pallas_call(
        functools.partial(_attn_kernel, block=block),
        grid=(b, MLA_HEADS),
        in_specs=[
            pl.BlockSpec((1, s, MLA_HEAD_W), lambda i, h: (i, 0, h)),
            pl.BlockSpec((1, s, MLA_HEAD_W), lambda i, h: (i, 0, h)),
            pl.BlockSpec((1, s, MLA_VD), lambda i, h: (i, 0, h)),
        ],
        out_specs=pl.BlockSpec((1, s, MLA_VD), lambda i, h: (i, 0, h)),
        out_shape=jax.ShapeDtypeStruct((b, s, MLA_HEADS * MLA_VD), MXU_DTYPE),
        scratch_shapes=[
            pltpu.VMEM((2, block, LANES), F32),
            pltpu.VMEM((2, block, MLA_VD + LANES), F32),
        ],
        compiler_params=_params("parallel", "parallel"),
        name="mla_attention",
    )(q, k, v)


def _tiles(seq):
    assert seq % RET_SUPER == 0, seq
    tm = 512 if seq % 512 == 0 else RET_SUPER
    mla_sub_tiles = 2 if seq % (2 * tm) == 0 else 1
    ret_block = 2 * tm if seq % (2 * tm) == 0 else tm
    return dict(tm=tm, ret_block=ret_block, attn_block=tm, ff_chunk=512, mla_sub_tiles=mla_sub_tiles)


def kernel(x, p, mix_norm, ret_w_in, ret_gn, ret_w_out, mla_w_in, mla_q_a_norm, mla_kv_a_norm, mla_w_uq,
           mla_w_ukv, mla_q_norm, mla_k_norm, mla_w_out, mlp_norm, mlp_w1, mlp_w2, ple_norm, ple_gate_w,
           ple_proj_w):
    b, s, d = x.shape
    depth = p.shape[0]
    t = b * s
    cfg = _tiles(s)
    tm = cfg["tm"]
    h = x.reshape(t, d)
    p3d = p.reshape(depth, t, -1)
    mlp_gain = mlp_norm.reshape(depth, 1, d)
    ple_gain = ple_norm.reshape(depth, 1, d)
    w1, w2, wg, wp = (w.astype(MXU_DTYPE) for w in (mlp_w1, mlp_w2, ple_gate_w, ple_proj_w))
    for i in range(depth):
        j = i // 2
        if i % 2 == 0:
            q, k, v, gate = _ret_proj(h, mix_norm[i], ret_w_in[j], ret_gn[j], s, tm)
            y = _retention(q.reshape(b, s, -1), k.reshape(b, s, -1), v.reshape(b, s, -1),
                           gate.reshape(b, s, -1), cfg["ret_block"])
            w_out = ret_w_out[j]
        else:
            q, k, v = _mla_proj(h, mix_norm[i], mla_w_in[j], mla_q_a_norm[j], mla_kv_a_norm[j], mla_w_uq[j],
                                mla_w_ukv[j], mla_q_norm[j], mla_k_norm[j], s, tm, cfg["mla_sub_tiles"])
            y = _attention(q.reshape(b, s, -1), k.reshape(b, s, -1), v.reshape(b, s, -1), cfg["attn_block"])
            w_out = mla_w_out[j]
        h = _tail(h, y.reshape(t, -1), p3d, w_out, mlp_gain, w1, w2, ple_gain, wg, wp, i, tm, cfg["ff_chunk"])
    return h.reshape(b, s, d)
```

```python
import functools

import jax
import jax.numpy as jnp
import numpy as np
from jax import lax
from jax.experimental import pallas as pl
from jax.experimental.pallas import tpu as pltpu

EPS = 1e-6
ROPE_THETA = 10000.0
CHUNK = 64
LANES = 128
MXU_WIDTH = 256

RET_HEADS = 4
RET_DK = 256
RET_DV = 512
RET_SUPER = 256

MLA_HEADS = 8
MLA_NOPE = 128
MLA_ROPE = 64
MLA_QKD = MLA_NOPE + MLA_ROPE
MLA_VD = 128
MLA_Q_RANK = 384
MLA_KV_RANK = 256
MLA_HEAD_W = 256

MXU_DTYPE = jnp.bfloat16
F32 = jnp.float32
LOG2_E = 1.4426950408889634
VMEM_LIMIT = 56 * 1024 * 1024

_NT = (((1,), (1,)), ((), ()))
_TN = (((0,), (0,)), ((), ()))


def _dot(a, b):
    return jnp.dot(a, b, preferred_element_type=F32)


def _rms_scale(x):
    return lax.rsqrt(jnp.mean(x * x, axis=-1, keepdims=True) + EPS)


def _rms_scale_lanes(x):
    return jnp.broadcast_to(_rms_scale(x), (x.shape[0], LANES))


def _row_sumsq(x, weights=None):
    if weights is None:
        weights = jnp.ones((x.shape[-1], LANES), MXU_DTYPE)
    return _dot((x * x).astype(MXU_DTYPE), weights)


def _lanes(a, width):
    return a if width == LANES else jnp.tile(a, (1, width // LANES))


def _sigmoid(x):
    return 0.5 * jnp.tanh(0.5 * x) + 0.5


def _resident(shape):
    zeros = (0,) * len(shape)
    return pl.BlockSpec(shape, lambda *_: zeros, pipeline_mode=pl.Buffered(1))


def _params(*semantics):
    return pltpu.CompilerParams(dimension_semantics=semantics, vmem_limit_bytes=VMEM_LIMIT)


def _rope_angles(seq, dim):
    inv = 1.0 / (ROPE_THETA ** (jnp.arange(0, dim, 2, dtype=F32) / dim))
    ang = jnp.arange(seq, dtype=F32)[:, None] * inv[None, :]
    return jnp.cos(ang), jnp.sin(ang)


def _ret_proj_kernel(x_ref, g_ref, w_ref, cos_ref, sin_ref, gn_ref, q_ref, k_ref, v_ref, gate_ref):
    x = x_ref[...]
    inv = _rms_scale_lanes(x)
    xg = (x * g_ref[...]).astype(MXU_DTYPE)
    inv_wide = _lanes(inv, MXU_WIDTH)
    cos = cos_ref[...] * inv
    sin = sin_ref[...] * inv
    half = RET_DK // 2
    qk_w = RET_HEADS * RET_DK
    v_w = RET_HEADS * RET_DV
    k_scale = RET_DK ** -0.5
    for lo in range(0, v_w, MXU_WIDTH):
        g = _dot(xg, w_ref[:, 2 * qk_w + v_w + lo:2 * qk_w + v_w + lo + MXU_WIDTH]) * inv_wide
        gate_ref[:, lo:lo + MXU_WIDTH] = (g * _sigmoid(g) * gn_ref[:, lo:lo + MXU_WIDTH]).astype(gate_ref.dtype)
        v_ref[:, lo:lo + MXU_WIDTH] = (
            _dot(xg, w_ref[:, 2 * qk_w + lo:2 * qk_w + lo + MXU_WIDTH]) * inv_wide).astype(v_ref.dtype)
    for h in range(RET_HEADS):
        lo = h * RET_DK
        r = _dot(xg, w_ref[:, lo:lo + RET_DK])
        x1, x2 = r[:, :half], r[:, half:]
        q_ref[:, lo:lo + half] = (x1 * cos - x2 * sin).astype(q_ref.dtype)
        q_ref[:, lo + half:lo + RET_DK] = (x2 * cos + x1 * sin).astype(q_ref.dtype)
        r = _dot(xg, w_ref[:, qk_w + lo:qk_w + lo + RET_DK])
        x1, x2 = r[:, :half], r[:, half:]
        k_ref[:, lo:lo + half] = ((x1 * cos - x2 * sin) * k_scale).astype(k_ref.dtype)
        k_ref[:, lo + half:lo + RET_DK] = ((x2 * cos + x1 * sin) * k_scale).astype(k_ref.dtype)


def _ret_proj(x2d, gain, w_in, gn_gain, seq, tm):
    t, d = x2d.shape
    qk_w = RET_HEADS * RET_DK
    v_w = RET_HEADS * RET_DV
    cos, sin = _rope_angles(seq, RET_DK)
    pos_blocks = seq // tm
    row = lambda i: (i, 0)
    pos = lambda i: (i % pos_blocks, 0)
    return pl.pallas_call(
        _ret_proj_kernel,
        grid=(t // tm,),
        in_specs=[
            pl.BlockSpec((tm, d), row),
            _resident((1, d)),
            _resident(w_in.shape),
            pl.BlockSpec((tm, RET_DK // 2), pos),
            pl.BlockSpec((tm, RET_DK // 2), pos),
            _resident((1, v_w)),
        ],
        out_specs=[
            pl.BlockSpec((tm, qk_w), row),
            pl.BlockSpec((tm, qk_w), row),
            pl.BlockSpec((tm, v_w), row),
            pl.BlockSpec((tm, v_w), row),
        ],
        out_shape=[
            jax.ShapeDtypeStruct((t, qk_w), MXU_DTYPE),
            jax.ShapeDtypeStruct((t, qk_w), MXU_DTYPE),
            jax.ShapeDtypeStruct((t, v_w), MXU_DTYPE),
            jax.ShapeDtypeStruct((t, v_w), MXU_DTYPE),
        ],
        compiler_params=_params("parallel"),
        name="ret_proj",
    )(x2d, gain.reshape(1, d), w_in.astype(MXU_DTYPE), cos, sin, gn_gain.reshape(1, v_w))


def _retention_kernel(q_ref, k_ref, v_ref, gate_ref, decay_ref, qd_ref, kd_ref, cd_ref, y_ref, state_ref,
                      *, groups):
    @pl.when(pl.program_id(1) == 0)
    def _():
        state_ref[...] = jnp.zeros_like(state_ref)

    def matmuls(c, h):
        rows = slice(c * RET_SUPER, (c + 1) * RET_SUPER)
        qk_cols = slice(h * RET_DK, (h + 1) * RET_DK)
        v_cols = slice(h * RET_DV, (h + 1) * RET_DV)
        q = q_ref[0, rows, qk_cols]
        k = k_ref[0, rows, qk_cols]
        v = v_ref[0, rows, v_cols]
        state = state_ref[h]
        scores = lax.dot_general(q, k, _NT, preferred_element_type=F32) * decay_ref[h]
        lhs = jnp.concatenate([q * qd_ref[h], scores.astype(MXU_DTYPE)], axis=1)
        rhs = jnp.concatenate([state.astype(MXU_DTYPE), v], axis=0)
        out = _dot(lhs, rhs)
        state_ref[h] = state * cd_ref[h] + lax.dot_general(k * kd_ref[h], v, _TN, preferred_element_type=F32)
        return out

    def finish(c, h, out):
        rows = slice(c * RET_SUPER, (c + 1) * RET_SUPER)
        v_cols = slice(h * RET_DV, (h + 1) * RET_DV)
        gate = gate_ref[0, rows, v_cols]
        y_ref[0, rows, v_cols] = gate * (out * _rms_scale(out)).astype(y_ref.dtype)

    units = [(c, h) for c in range(groups) for h in range(RET_HEADS)]
    ahead = matmuls(*units[0])
    for u, unit in enumerate(units):
        out = ahead
        if u + 1 < len(units):
            ahead = matmuls(*units[u + 1])
        finish(*unit, out)


def _retention_tables():
    log_gamma = np.log(1.0 - 2.0 ** (-5.0 - np.arange(RET_HEADS, dtype=np.float64)))
    idx = np.arange(RET_SUPER, dtype=np.float64)
    dist = np.abs(idx[:, None] - idx[None, :])
    chunk = np.arange(RET_SUPER) // CHUNK
    visible = chunk[None, :] <= chunk[:, None]
    decay = np.where(visible[None], np.exp(log_gamma[:, None, None] * dist), 0.0)
    q_decay = np.exp(log_gamma[:, None] * (idx + 1.0))
    k_decay = np.exp(log_gamma[:, None] * (RET_SUPER - 1.0 - idx))
    group_decay = np.exp(log_gamma * RET_SUPER)
    qd = np.broadcast_to(q_decay[:, :, None], (RET_HEADS, RET_SUPER, RET_DK))
    kd = np.broadcast_to(k_decay[:, :, None], (RET_HEADS, RET_SUPER, RET_DK))
    cd = np.broadcast_to(group_decay[:, None, None], (RET_HEADS, 1, RET_DV))
    decay, qd, kd, cd = (jnp.asarray(np.ascontiguousarray(a), F32) for a in (decay, qd, kd, cd))
    return decay, qd.astype(MXU_DTYPE), kd.astype(MXU_DTYPE), cd


def _retention(q, k, v, gate, block):
    b, s, _ = q.shape
    qk_w = RET_HEADS * RET_DK
    v_w = RET_HEADS * RET_DV
    decay, qd, kd, cd = _retention_tables()
    tok = lambda i, j: (i, j, 0)
    return pl.pallas_call(
        functools.partial(_retention_kernel, groups=block // RET_SUPER),
        grid=(b, s // block),
        in_specs=[
            pl.BlockSpec((1, block, qk_w), tok),
            pl.BlockSpec((1, block, qk_w), tok),
            pl.BlockSpec((1, block, v_w), tok),
            pl.BlockSpec((1, block, v_w), tok),
            _resident(decay.shape),
            _resident(qd.shape),
            _resident(kd.shape),
            _resident(cd.shape),
        ],
        out_specs=pl.BlockSpec((1, block, v_w), tok),
        out_shape=jax.ShapeDtypeStruct((b, s, v_w), MXU_DTYPE),
        scratch_shapes=[pltpu.VMEM((RET_HEADS, RET_DK, RET_DV), F32)],
        compiler_params=_params("parallel", "arbitrary"),
        name="retention",
    )(q, k, v, gate, decay, qd, kd, cd)


def _tail_kernel(h_ref, y_ref, p_ref, wo_ref, mg_ref, w1_ref, w2_ref, pg_ref, wg_ref, wp_ref, o_ref,
                 *, ff_chunk):
    h = h_ref[...] + _dot(y_ref[...], wo_ref[...])
    d = h.shape[-1]
    inv = _lanes(_rms_scale_lanes(h), ff_chunk)
    hg = (h * mg_ref[...]).astype(MXU_DTYPE)
    d_ff = w1_ref.shape[1]
    mlp = None
    for c in range(d_ff // ff_chunk):
        cols = slice(c * ff_chunk, (c + 1) * ff_chunk)
        a = jnp.square(jnp.maximum(_dot(hg, w1_ref[:, cols]) * inv, 0.0))
        part = _dot(a.astype(MXU_DTYPE), w2_ref[cols, :])
        mlp = part if mlp is None else mlp + part
    h = h + mlp
    inv = _lanes(_rms_scale_lanes(h), MXU_WIDTH)
    hg = (h * pg_ref[...]).astype(MXU_DTYPE)
    pe = p_ref[...].astype(MXU_DTYPE)
    for lo in range(0, d, MXU_WIDTH):
        cols = slice(lo, lo + MXU_WIDTH)
        gate = _sigmoid(_dot(hg, wg_ref[:, cols]) * inv)
        o_ref[:, cols] = h[:, cols] + gate * _dot(pe, wp_ref[:, cols])


def _layer_of(shape, layer):
    index = (layer,) + (0,) * (len(shape) - 1)
    return pl.BlockSpec((None,) + tuple(shape[1:]), lambda *_: index, pipeline_mode=pl.Buffered(1))


def _tail(h2d, y2d, p3d, w_out, mlp_gain, w1, w2, ple_gain, wg, wp, layer, tm, ff_chunk):
    t, d = h2d.shape
    row = lambda i: (i, 0)
    return pl.pallas_call(
        functools.partial(_tail_kernel, ff_chunk=ff_chunk),
        grid=(t // tm,),
        in_specs=[
            pl.BlockSpec((tm, d), row),
            pl.BlockSpec((tm, y2d.shape[1]), row),
            pl.BlockSpec((None, tm, p3d.shape[2]), lambda i: (layer, i, 0)),
            _resident(w_out.shape),
            _layer_of(mlp_gain.shape, layer),
            _layer_of(w1.shape, layer),
            _layer_of(w2.shape, layer),
            _layer_of(ple_gain.shape, layer),
            _layer_of(wg.shape, layer),
            _layer_of(wp.shape, layer),
        ],
        out_specs=pl.BlockSpec((tm, d), row),
        out_shape=jax.ShapeDtypeStruct((t, d), F32),
        compiler_params=_params("parallel"),
        name="layer_tail",
    )(h2d, y2d, p3d, w_out.astype(MXU_DTYPE), mlp_gain, w1, w2, ple_gain, wg, wp)


def _mla_proj_kernel(x_ref, g_ref, win_ref, qa_ref, kva_ref, wuq_ref, wukv_ref, qn_ref, kn_ref,
                     qrope_ref, krope_ref, q_ref, k_ref, v_ref, *, sub_tiles):
    tm = x_ref.shape[0] // sub_tiles
    d = x_ref.shape[-1]
    lane = lax.broadcasted_iota(jnp.int32, (1, LANES), 1)
    first_half = (lane < MLA_ROPE).astype(F32)
    softmax_scale = MLA_QKD ** -0.5 * LOG2_E
    qn = qn_ref[...]
    kn = kn_ref[...]
    ones = jnp.ones((MLA_NOPE, LANES), MXU_DTYPE)
    halves = jnp.full((LANES, LANES), 0.5, MXU_DTYPE)
    head_weights = jnp.concatenate([ones, halves], axis=0)

    def down(rows):
        x = x_ref[rows, :]
        inv = lax.rsqrt(_row_sumsq(x) * (1.0 / d) + EPS)
        proj = _dot((x * g_ref[...]).astype(MXU_DTYPE), win_ref[...])
        proj = proj * _lanes(inv, proj.shape[-1])
        c_q = proj[:, :MLA_Q_RANK]
        c_kv = proj[:, MLA_Q_RANK:MLA_Q_RANK + MLA_KV_RANK]
        k_r = proj[:, MLA_Q_RANK + MLA_KV_RANK:]
        inv = lax.rsqrt(_row_sumsq(c_q) * (1.0 / MLA_Q_RANK) + EPS)
        c_q = (c_q * _lanes(inv, MLA_Q_RANK) * qa_ref[...]).astype(MXU_DTYPE)
        inv = lax.rsqrt(_row_sumsq(c_kv) * (1.0 / MLA_KV_RANK) + EPS)
        c_kv = (c_kv * _lanes(inv, MLA_KV_RANK) * kva_ref[...]).astype(MXU_DTYPE)
        k_rot = k_r * krope_ref[rows, :]
        k_rot = (k_rot + pltpu.roll(k_rot, MLA_ROPE, 1)) * first_half
        return c_q, c_kv, k_rot, _row_sumsq(k_r, halves)

    tiles = [slice(i * tm, (i + 1) * tm) for i in range(sub_tiles)]
    projected = [down(rows) for rows in tiles]

    def up_project(t, h):
        c_q, c_kv = projected[t][:2]
        cols = slice(h * MLA_HEAD_W, (h + 1) * MLA_HEAD_W)
        return _dot(c_q, wuq_ref[:, cols]), _dot(c_kv, wukv_ref[:, cols])

    def normalise(t, h, qh, kvh):
        rows = tiles[t]
        k_rot, k_r_ss = projected[t][2:]
        lo = h * MLA_HEAD_W
        q_nope, q_r = qh[:, :MLA_NOPE], qh[:, MLA_NOPE:]
        inv = lax.rsqrt(_row_sumsq(qh, head_weights) * (1.0 / MLA_QKD) + EPS) * softmax_scale
        q_rot = q_r * qrope_ref[rows, :]
        q_rot = q_rot + pltpu.roll(q_rot, MLA_ROPE, 1)
        q_ref[rows, lo:lo + MLA_NOPE] = (q_nope * inv * qn).astype(q_ref.dtype)
        q_ref[rows, lo + MLA_NOPE:lo + MLA_HEAD_W] = (q_rot * inv).astype(q_ref.dtype)

        k_nope, v = kvh[:, :MLA_NOPE], kvh[:, MLA_NOPE:]
        inv = lax.rsqrt((_row_sumsq(k_nope, ones) + k_r_ss) * (1.0 / MLA_QKD) + EPS)
        k_ref[rows, lo:lo + MLA_NOPE] = (k_nope * inv * kn).astype(k_ref.dtype)
        k_ref[rows, lo + MLA_NOPE:lo + MLA_HEAD_W] = (k_rot * inv).astype(k_ref.dtype)
        v_ref[rows, h * MLA_VD:(h + 1) * MLA_VD] = v.astype(v_ref.dtype)

    units = [(t, h) for t in range(sub_tiles) for h in range(MLA_HEADS)]
    ahead = up_project(*units[0])
    for u, unit in enumerate(units):
        current = ahead
        if u + 1 < len(units):
            ahead = up_project(*units[u + 1])
        normalise(*unit, *current)


def _rotate_half_cols(w):
    half = w.shape[-1] // 2
    return jnp.concatenate([-w[..., half:], w[..., :half]], axis=-1)


def _rope_gain_table(seq, gain_rope):
    cos, sin = _rope_angles(seq, MLA_ROPE)
    base = jnp.concatenate([cos, cos, sin, sin], axis=-1)
    half = MLA_ROPE // 2
    gains = jnp.concatenate([gain_rope, gain_rope[half:], gain_rope[:half]])
    return base * gains[None, :]


def _mla_proj(x2d, gain, w_in, q_a_gain, kv_a_gain, w_uq, w_ukv, q_gain, k_gain, seq, tm, sub_tiles):
    t, d = x2d.shape
    kr_lo = MLA_Q_RANK + MLA_KV_RANK
    w_in_ext = jnp.concatenate([w_in, _rotate_half_cols(w_in[:, kr_lo:])], axis=1)
    w_uq_h = w_uq.reshape(MLA_Q_RANK, MLA_HEADS, MLA_QKD)
    w_uq_ext = jnp.concatenate([w_uq_h, _rotate_half_cols(w_uq_h[..., MLA_NOPE:])], axis=-1)
    w_uq_ext = w_uq_ext.reshape(MLA_Q_RANK, MLA_HEADS * MLA_HEAD_W)
    q_rope = _rope_gain_table(seq, q_gain[MLA_NOPE:])
    k_rope = _rope_gain_table(seq, k_gain[MLA_NOPE:])
    qk_w = MLA_HEADS * MLA_HEAD_W
    v_w = MLA_HEADS * MLA_VD
    block = sub_tiles * tm
    pos_blocks = seq // block
    row = lambda i: (i, 0)
    pos = lambda i: (i % pos_blocks, 0)
    return pl.pallas_call(
        functools.partial(_mla_proj_kernel, sub_tiles=sub_tiles),
        grid=(t // block,),
        in_specs=[
            pl.BlockSpec((block, d), row),
            _resident((1, d)),
            _resident(w_in_ext.shape),
            _resident((1, MLA_Q_RANK)),
            _resident((1, MLA_KV_RANK)),
            _resident(w_uq_ext.shape),
            _resident(w_ukv.shape),
            _resident((1, MLA_NOPE)),
            _resident((1, MLA_NOPE)),
            pl.BlockSpec((block, LANES), pos),
            pl.BlockSpec((block, LANES), pos),
        ],
        out_specs=[
            pl.BlockSpec((block, qk_w), row),
            pl.BlockSpec((block, qk_w), row),
            pl.BlockSpec((block, v_w), row),
        ],
        out_shape=[
            jax.ShapeDtypeStruct((t, qk_w), MXU_DTYPE),
            jax.ShapeDtypeStruct((t, qk_w), MXU_DTYPE),
            jax.ShapeDtypeStruct((t, v_w), MXU_DTYPE),
        ],
        compiler_params=_params("parallel"),
        name="mla_proj",
    )(x2d, gain.reshape(1, d), w_in_ext.astype(MXU_DTYPE), q_a_gain.reshape(1, -1), kv_a_gain.reshape(1, -1),
      w_uq_ext.astype(MXU_DTYPE), w_ukv.astype(MXU_DTYPE), q_gain[:MLA_NOPE].reshape(1, -1),
      k_gain[:MLA_NOPE].reshape(1, -1), q_rope, k_rope)


def _attn_kernel(q_ref, k_ref, v_ref, o_ref, m_ref, acc_ref, *, block):
    n_blocks = q_ref.shape[1] // block
    half = block // 2

    units = []
    for qb in range(n_blocks):
        q0 = qb * block
        units += [((q0, block), (kb * block, block)) for kb in range(qb)]
        units += [((q0, half), (q0, half)), ((q0 + half, half), (q0, block))]

    def span(start_size):
        return slice(start_size[0], start_size[0] + start_size[1])

    def scores(q_rows, k_rows):
        return lax.dot_general(q_ref[0, span(q_rows), :], k_ref[0, span(k_rows), :], _NT,
                               preferred_element_type=F32)

    def update(s, q_rows, k_rows):
        (q0, nq), (k0, nk) = q_rows, k_rows
        slot = (q0 // block) % 2
        local = slice(q0 % block, q0 % block + nq)
        first = k0 == 0
        last = k0 + nk == q0 + nq
        if last:
            q_chunk = (q0 + lax.broadcasted_iota(jnp.int32, s.shape, 0)) // CHUNK
            k_chunk = (k0 + lax.broadcasted_iota(jnp.int32, s.shape, 1)) // CHUNK
            s = jnp.where(k_chunk <= q_chunk, s, -1e30)
        m_new = jnp.broadcast_to(jnp.max(s, axis=-1, keepdims=True), (nq, LANES))
        if not first:
            m_prev = m_ref[slot, local, :]
            m_new = jnp.maximum(m_prev, m_new)
        p = jnp.exp2(s - jnp.tile(m_new, (1, nk // LANES)))
        ones = jnp.ones((nk, LANES), MXU_DTYPE)
        acc = _dot(p.astype(MXU_DTYPE), jnp.concatenate([v_ref[0, span(k_rows), :], ones], axis=1))
        if not first:
            acc = jnp.tile(jnp.exp2(m_prev - m_new), (1, 2)) * acc_ref[slot, local, :] + acc
        if last:
            o_ref[0, span(q_rows), :] = (acc[:, :MLA_VD] / acc[:, MLA_VD:]).astype(o_ref.dtype)
        else:
            m_ref[slot, local, :] = m_new
            acc_ref[slot, local, :] = acc

    s_next = scores(*units[0])
    for u, unit in enumerate(units):
        s = s_next
        if u + 1 < len(units):
            s_next = scores(*units[u + 1])
        update(s, *unit)


def _attention(q, k, v, block):
    b, s, _ = q.shape
    return pl.pallas_call(
        functools.partial(_attn_kernel, block=block),
        grid=(b, MLA_HEADS),
        in_specs=[
            pl.BlockSpec((1, s, MLA_HEAD_W), lambda i, h: (i, 0, h)),
            pl.BlockSpec((1, s, MLA_HEAD_W), lambda i, h: (i, 0, h)),
            pl.BlockSpec((1, s, MLA_VD), lambda i, h: (i, 0, h)),
        ],
        out_specs=pl.BlockSpec((1, s, MLA_VD), lambda i, h: (i, 0, h)),
        out_shape=jax.ShapeDtypeStruct((b, s, MLA_HEADS * MLA_VD), MXU_DTYPE),
        scratch_shapes=[
            pltpu.VMEM((2, block, LANES), F32),
            pltpu.VMEM((2, block, MLA_VD + LANES), F32),
        ],
        compiler_params=_params("parallel", "parallel"),
        name="mla_attention",
    )(q, k, v)


def _tiles(seq):
    assert seq % RET_SUPER == 0, seq
    tm = 512 if seq % 512 == 0 else RET_SUPER
    mla_sub_tiles = 2 if seq % (2 * tm) == 0 else 1
    ret_block = 2 * tm if seq % (2 * tm) == 0 else tm
    return dict(tm=tm, ret_block=ret_block, attn_block=tm, ff_chunk=512, mla_sub_tiles=mla_sub_tiles)


def kernel(x, p, mix_norm, ret_w_in, ret_gn, ret_w_out, mla_w_in, mla_q_a_norm, mla_kv_a_norm, mla_w_uq,
           mla_w_ukv, mla_q_norm, mla_k_norm, mla_w_out, mlp_norm, mlp_w1, mlp_w2, ple_norm, ple_gate_w,
           ple_proj_w):
    b, s, d = x.shape
    depth = p.shape[0]
    t = b * s
    cfg = _tiles(s)
    tm = cfg["tm"]
    h = x.reshape(t, d)
    p3d = p.reshape(depth, t, -1)
    mlp_gain = mlp_norm.reshape(depth, 1, d)
    ple_gain = ple_norm.reshape(depth, 1, d)
    w1, w2, wg, wp = (w.astype(MXU_DTYPE) for w in (mlp_w1, mlp_w2, ple_gate_w, ple_proj_w))
    for i in range(depth):
        j = i // 2
        if i % 2 == 0:
            q, k, v, gate = _ret_proj(h, mix_norm[i], ret_w_in[j], ret_gn[j], s, cfg["ret_block"])
            y = _retention(q.reshape(b, s, -1), k.reshape(b, s, -1), v.reshape(b, s, -1),
                           gate.reshape(b, s, -1), cfg["ret_block"])
            w_out = ret_w_out[j]
        else:
            q, k, v = _mla_proj(h, mix_norm[i], mla_w_in[j], mla_q_a_norm[j], mla_kv_a_norm[j], mla_w_uq[j],
                                mla_w_ukv[j], mla_q_norm[j], mla_k_norm[j], s, tm, cfg["mla_sub_tiles"])
            y = _attention(q.reshape(b, s, -1), k.reshape(b, s, -1), v.reshape(b, s, -1), cfg["attn_block"])
            w_out = mla_w_out[j]
        h = _tail(h, y.reshape(t, -1), p3d, w_out, mlp_gain, w1, w2, ple_gain, wg, wp, i, tm, cfg["ff_chunk"])
    return h.reshape(b, s, d)
```

```python
import functools

import jax
import jax.numpy as jnp
import numpy as np
from jax import lax
from jax.experimental import pallas as pl
from jax.experimental.pallas import tpu as pltpu

EPS = 1e-6
ROPE_THETA = 10000.0
CHUNK = 64
LANES = 128
MXU_WIDTH = 256

RET_HEADS = 4
RET_DK = 256
RET_DV = 512
RET_SUPER = 256

MLA_HEADS = 8
MLA_NOPE = 128
MLA_ROPE = 64
MLA_QKD = MLA_NOPE + MLA_ROPE
MLA_VD = 128
MLA_Q_RANK = 384
MLA_KV_RANK = 256
MLA_HEAD_W = 256

MXU_DTYPE = jnp.bfloat16
F32 = jnp.float32
LOG2_E = 1.4426950408889634
VMEM_LIMIT = 56 * 1024 * 1024

_NT = (((1,), (1,)), ((), ()))
_TN = (((0,), (0,)), ((), ()))


def _dot(a, b):
    return jnp.dot(a, b, preferred_element_type=F32)


def _rms_scale(x):
    return lax.rsqrt(jnp.mean(x * x, axis=-1, keepdims=True) + EPS)


def _rms_scale_lanes(x):
    return jnp.broadcast_to(_rms_scale(x), (x.shape[0], LANES))


def _row_sumsq(x, weights=None):
    if weights is None:
        weights = jnp.ones((x.shape[-1], LANES), MXU_DTYPE)
    return _dot((x * x).astype(MXU_DTYPE), weights)


def _lanes(a, width):
    return a if width == LANES else jnp.tile(a, (1, width // LANES))


def _sigmoid(x):
    return 0.5 * jnp.tanh(0.5 * x) + 0.5


def _resident(shape):
    zeros = (0,) * len(shape)
    return pl.BlockSpec(shape, lambda *_: zeros, pipeline_mode=pl.Buffered(1))


def _params(*semantics):
    return pltpu.CompilerParams(dimension_semantics=semantics, vmem_limit_bytes=VMEM_LIMIT)


def _rope_angles(seq, dim):
    inv = 1.0 / (ROPE_THETA ** (jnp.arange(0, dim, 2, dtype=F32) / dim))
    ang = jnp.arange(seq, dtype=F32)[:, None] * inv[None, :]
    return jnp.cos(ang), jnp.sin(ang)


def _ret_proj_kernel(x_ref, g_ref, w_ref, cos_ref, sin_ref, gn_ref, q_ref, k_ref, v_ref, gate_ref):
    x = x_ref[...]
    inv = _rms_scale_lanes(x)
    xg = (x * g_ref[...]).astype(MXU_DTYPE)
    inv_wide = _lanes(inv, MXU_WIDTH)
    cos = cos_ref[...] * inv
    sin = sin_ref[...] * inv
    half = RET_DK // 2
    qk_w = RET_HEADS * RET_DK
    v_w = RET_HEADS * RET_DV
    k_scale = RET_DK ** -0.5
    for lo in range(0, v_w, MXU_WIDTH):
        g = _dot(xg, w_ref[:, 2 * qk_w + v_w + lo:2 * qk_w + v_w + lo + MXU_WIDTH]) * inv_wide
        gate_ref[:, lo:lo + MXU_WIDTH] = (g * _sigmoid(g) * gn_ref[:, lo:lo + MXU_WIDTH]).astype(gate_ref.dtype)
        v_ref[:, lo:lo + MXU_WIDTH] = (
            _dot(xg, w_ref[:, 2 * qk_w + lo:2 * qk_w + lo + MXU_WIDTH]) * inv_wide).astype(v_ref.dtype)
    for h in range(RET_HEADS):
        lo = h * RET_DK
        r = _dot(xg, w_ref[:, lo:lo + RET_DK])
        x1, x2 = r[:, :half], r[:, half:]
        q_ref[:, lo:lo + half] = (x1 * cos - x2 * sin).astype(q_ref.dtype)
        q_ref[:, lo + half:lo + RET_DK] = (x2 * cos + x1 * sin).astype(q_ref.dtype)
        r = _dot(xg, w_ref[:, qk_w + lo:qk_w + lo + RET_DK])
        x1, x2 = r[:, :half], r[:, half:]
        k_ref[:, lo:lo + half] = ((x1 * cos - x2 * sin) * k_scale).astype(k_ref.dtype)
        k_ref[:, lo + half:lo + RET_DK] = ((x2 * cos + x1 * sin) * k_scale).astype(k_ref.dtype)


def _ret_proj(x2d, gain, w_in, gn_gain, seq, tm):
    t, d = x2d.shape
    qk_w = RET_HEADS * RET_DK
    v_w = RET_HEADS * RET_DV
    cos, sin = _rope_angles(seq, RET_DK)
    pos_blocks = seq // tm
    row = lambda i: (i, 0)
    pos = lambda i: (i % pos_blocks, 0)
    return pl.pallas_call(
        _ret_proj_kernel,
        grid=(t // tm,),
        in_specs=[
            pl.BlockSpec((tm, d), row),
            _resident((1, d)),
            _resident(w_in.shape),
            pl.BlockSpec((tm, RET_DK // 2), pos),
            pl.BlockSpec((tm, RET_DK // 2), pos),
            _resident((1, v_w)),
        ],
        out_specs=[
            pl.BlockSpec((tm, qk_w), row),
            pl.BlockSpec((tm, qk_w), row),
            pl.BlockSpec((tm, v_w), row),
            pl.BlockSpec((tm, v_w), row),
        ],
        out_shape=[
            jax.ShapeDtypeStruct((t, qk_w), MXU_DTYPE),
            jax.ShapeDtypeStruct((t, qk_w), MXU_DTYPE),
            jax.ShapeDtypeStruct((t, v_w), MXU_DTYPE),
            jax.ShapeDtypeStruct((t, v_w), MXU_DTYPE),
        ],
        compiler_params=_params("parallel"),
        name="ret_proj",
    )(x2d, gain.reshape(1, d), w_in.astype(MXU_DTYPE), cos, sin, gn_gain.reshape(1, v_w))


def _retention_kernel(q_ref, k_ref, v_ref, decay_ref, qd_ref, kd_ref, cd_ref, y_ref, state_ref, *, groups):
    @pl.when(pl.program_id(1) == 0)
    def _():
        state_ref[...] = jnp.zeros_like(state_ref)

    def matmuls(c, h):
        rows = slice(c * RET_SUPER, (c + 1) * RET_SUPER)
        qk_cols = slice(h * RET_DK, (h + 1) * RET_DK)
        v_cols = slice(h * RET_DV, (h + 1) * RET_DV)
        q = q_ref[0, rows, qk_cols]
        k = k_ref[0, rows, qk_cols]
        v = v_ref[0, rows, v_cols]
        state = state_ref[h]
        scores = lax.dot_general(q, k, _NT, preferred_element_type=F32) * decay_ref[h]
        lhs = jnp.concatenate([q * qd_ref[h], scores.astype(MXU_DTYPE)], axis=1)
        rhs = jnp.concatenate([state.astype(MXU_DTYPE), v], axis=0)
        out = _dot(lhs, rhs)
        state_ref[h] = state * cd_ref[h] + lax.dot_general(k * kd_ref[h], v, _TN, preferred_element_type=F32)
        return out

    def finish(c, h, out):
        rows = slice(c * RET_SUPER, (c + 1) * RET_SUPER)
        v_cols = slice(h * RET_DV, (h + 1) * RET_DV)
        y_ref[0, rows, v_cols] = out.astype(y_ref.dtype)

    units = [(c, h) for c in range(groups) for h in range(RET_HEADS)]
    ahead = matmuls(*units[0])
    for u, unit in enumerate(units):
        out = ahead
        if u + 1 < len(units):
            ahead = matmuls(*units[u + 1])
        finish(*unit, out)


def _retention_tables():
    log_gamma = np.log(1.0 - 2.0 ** (-5.0 - np.arange(RET_HEADS, dtype=np.float64)))
    idx = np.arange(RET_SUPER, dtype=np.float64)
    dist = np.abs(idx[:, None] - idx[None, :])
    chunk = np.arange(RET_SUPER) // CHUNK
    visible = chunk[None, :] <= chunk[:, None]
    decay = np.where(visible[None], np.exp(log_gamma[:, None, None] * dist), 0.0)
    q_decay = np.exp(log_gamma[:, None] * (idx + 1.0))
    k_decay = np.exp(log_gamma[:, None] * (RET_SUPER - 1.0 - idx))
    group_decay = np.exp(log_gamma * RET_SUPER)
    qd = np.broadcast_to(q_decay[:, :, None], (RET_HEADS, RET_SUPER, RET_DK))
    kd = np.broadcast_to(k_decay[:, :, None], (RET_HEADS, RET_SUPER, RET_DK))
    cd = np.broadcast_to(group_decay[:, None, None], (RET_HEADS, 1, RET_DV))
    decay, qd, kd, cd = (jnp.asarray(np.ascontiguousarray(a), F32) for a in (decay, qd, kd, cd))
    return decay, qd.astype(MXU_DTYPE), kd.astype(MXU_DTYPE), cd


def _retention(q, k, v, block):
    b, s, _ = q.shape
    qk_w = RET_HEADS * RET_DK
    v_w = RET_HEADS * RET_DV
    decay, qd, kd, cd = _retention_tables()
    tok = lambda i, j: (i, j, 0)
    return pl.pallas_call(
        functools.partial(_retention_kernel, groups=block // RET_SUPER),
        grid=(b, s // block),
        in_specs=[
            pl.BlockSpec((1, block, qk_w), tok),
            pl.BlockSpec((1, block, qk_w), tok),
            pl.BlockSpec((1, block, v_w), tok),
            _resident(decay.shape),
            _resident(qd.shape),
            _resident(kd.shape),
            _resident(cd.shape),
        ],
        out_specs=pl.BlockSpec((1, block, v_w), tok),
        out_shape=jax.ShapeDtypeStruct((b, s, v_w), MXU_DTYPE),
        scratch_shapes=[pltpu.VMEM((RET_HEADS, RET_DK, RET_DV), F32)],
        compiler_params=_params("parallel", "arbitrary"),
        name="retention",
    )(q, k, v, decay, qd, kd, cd)


def _tail_kernel(*refs, ff_chunk, gated_heads):
    if gated_heads:
        h_ref, y_ref, gate_ref, p_ref, wo_ref, mg_ref, w1_ref, w2_ref, pg_ref, wg_ref, wp_ref, o_ref = refs
    else:
        h_ref, y_ref, p_ref, wo_ref, mg_ref, w1_ref, w2_ref, pg_ref, wg_ref, wp_ref, o_ref = refs
    h = h_ref[...]
    d = h.shape[-1]
    if gated_heads:
        width = y_ref.shape[1] // gated_heads
        for head in range(gated_heads):
            cols = slice(head * width, (head + 1) * width)
            y = y_ref[:, cols]
            inv = _rms_scale_lanes(y.astype(F32))
            h = h + _dot(gate_ref[:, cols] * y, wo_ref[cols, :]) * _lanes(inv, d)
    else:
        h = h + _dot(y_ref[...], wo_ref[...])
    inv = _lanes(_rms_scale_lanes(h), ff_chunk)
    hg = (h * mg_ref[...]).astype(MXU_DTYPE)
    d_ff = w1_ref.shape[1]
    mlp = None
    for c in range(d_ff // ff_chunk):
        cols = slice(c * ff_chunk, (c + 1) * ff_chunk)
        a = jnp.square(jnp.maximum(_dot(hg, w1_ref[:, cols]) * inv, 0.0))
        part = _dot(a.astype(MXU_DTYPE), w2_ref[cols, :])
        mlp = part if mlp is None else mlp + part
    h = h + mlp
    inv = _lanes(_rms_scale_lanes(h), MXU_WIDTH)
    hg = (h * pg_ref[...]).astype(MXU_DTYPE)
    pe = p_ref[...].astype(MXU_DTYPE)
    for lo in range(0, d, MXU_WIDTH):
        cols = slice(lo, lo + MXU_WIDTH)
        gate = _sigmoid(_dot(hg, wg_ref[:, cols]) * inv)
        o_ref[:, cols] = h[:, cols] + gate * _dot(pe, wp_ref[:, cols])


def _layer_of(shape, layer):
    index = (layer,) + (0,) * (len(shape) - 1)
    return pl.BlockSpec((None,) + tuple(shape[1:]), lambda *_: index, pipeline_mode=pl.Buffered(1))


def _tail(h2d, y2d, gate2d, p3d, w_out, mlp_gain, w1, w2, ple_gain, wg, wp, layer, tm, ff_chunk):
    t, d = h2d.shape
    row = lambda i: (i, 0)
    mixer = [y2d] if gate2d is None else [y2d, gate2d]
    return pl.pallas_call(
        functools.partial(_tail_kernel, ff_chunk=ff_chunk, gated_heads=0 if gate2d is None else RET_HEADS),
        grid=(t // tm,),
        in_specs=[
            pl.BlockSpec((tm, d), row),
            *[pl.BlockSpec((tm, a.shape[1]), row) for a in mixer],
            pl.BlockSpec((None, tm, p3d.shape[2]), lambda i: (layer, i, 0)),
            _resident(w_out.shape),
            _layer_of(mlp_gain.shape, layer),
            _layer_of(w1.shape, layer),
            _layer_of(w2.shape, layer),
            _layer_of(ple_gain.shape, layer),
            _layer_of(wg.shape, layer),
            _layer_of(wp.shape, layer),
        ],
        out_specs=pl.BlockSpec((tm, d), row),
        out_shape=jax.ShapeDtypeStruct((t, d), F32),
        compiler_params=_params("parallel"),
        name="layer_tail",
    )(h2d, *mixer, p3d, w_out.astype(MXU_DTYPE), mlp_gain, w1, w2, ple_gain, wg, wp)


def _mla_proj_kernel(x_ref, g_ref, win_ref, qa_ref, kva_ref, wuq_ref, wukv_ref, qn_ref, kn_ref,
                     qrope_ref, krope_ref, q_ref, k_ref, v_ref, *, sub_tiles):
    tm = x_ref.shape[0] // sub_tiles
    d = x_ref.shape[-1]
    lane = lax.broadcasted_iota(jnp.int32, (1, LANES), 1)
    first_half = (lane < MLA_ROPE).astype(F32)
    softmax_scale = MLA_QKD ** -0.5 * LOG2_E
    qn = qn_ref[...]
    kn = kn_ref[...]
    ones = jnp.ones((MLA_NOPE, LANES), MXU_DTYPE)
    halves = jnp.full((LANES, LANES), 0.5, MXU_DTYPE)
    head_weights = jnp.concatenate([ones, halves], axis=0)

    def down(rows):
        x = x_ref[rows, :]
        inv = lax.rsqrt(_row_sumsq(x) * (1.0 / d) + EPS)
        proj = _dot((x * g_ref[...]).astype(MXU_DTYPE), win_ref[...])
        proj = proj * _lanes(inv, proj.shape[-1])
        c_q = proj[:, :MLA_Q_RANK]
        c_kv = proj[:, MLA_Q_RANK:MLA_Q_RANK + MLA_KV_RANK]
        k_r = proj[:, MLA_Q_RANK + MLA_KV_RANK:]
        inv = lax.rsqrt(_row_sumsq(c_q) * (1.0 / MLA_Q_RANK) + EPS)
        c_q = (c_q * _lanes(inv, MLA_Q_RANK) * qa_ref[...]).astype(MXU_DTYPE)
        inv = lax.rsqrt(_row_sumsq(c_kv) * (1.0 / MLA_KV_RANK) + EPS)
        c_kv = (c_kv * _lanes(inv, MLA_KV_RANK) * kva_ref[...]).astype(MXU_DTYPE)
        k_rot = k_r * krope_ref[rows, :]
        k_rot = (k_rot + pltpu.roll(k_rot, MLA_ROPE, 1)) * first_half
        return c_q, c_kv, k_rot, _row_sumsq(k_r, halves)

    tiles = [slice(i * tm, (i + 1) * tm) for i in range(sub_tiles)]
    projected = [down(rows) for rows in tiles]

    def up_project(t, h):
        c_q, c_kv = projected[t][:2]
        cols = slice(h * MLA_HEAD_W, (h + 1) * MLA_HEAD_W)
        return _dot(c_q, wuq_ref[:, cols]), _dot(c_kv, wukv_ref[:, cols])

    def normalise(t, h, qh, kvh):
        rows = tiles[t]
        k_rot, k_r_ss = projected[t][2:]
        lo = h * MLA_HEAD_W
        q_nope, q_r = qh[:, :MLA_NOPE], qh[:, MLA_NOPE:]
        inv = lax.rsqrt(_row_sumsq(qh, head_weights) * (1.0 / MLA_QKD) + EPS) * softmax_scale
        q_rot = q_r * qrope_ref[rows, :]
        q_rot = q_rot + pltpu.roll(q_rot, MLA_ROPE, 1)
        q_ref[rows, lo:lo + MLA_NOPE] = (q_nope * inv * qn).astype(q_ref.dtype)
        q_ref[rows, lo + MLA_NOPE:lo + MLA_HEAD_W] = (q_rot * inv).astype(q_ref.dtype)

        k_nope, v = kvh[:, :MLA_NOPE], kvh[:, MLA_NOPE:]
        inv = lax.rsqrt((_row_sumsq(k_nope, ones) + k_r_ss) * (1.0 / MLA_QKD) + EPS)
        k_ref[rows, lo:lo + MLA_NOPE] = (k_nope * inv * kn).astype(k_ref.dtype)
        k_ref[rows, lo + MLA_NOPE:lo + MLA_HEAD_W] = (k_rot * inv).astype(k_ref.dtype)
        v_ref[rows, h * MLA_VD:(h + 1) * MLA_VD] = v.astype(v_ref.dtype)

    units = [(t, h) for t in range(sub_tiles) for h in range(MLA_HEADS)]
    ahead = up_project(*units[0])
    for u, unit in enumerate(units):
        current = ahead
        if u + 1 < len(units):
            ahead = up_project(*units[u + 1])
        normalise(*unit, *current)


def _rotate_half_cols(w):
    half = w.shape[-1] // 2
    return jnp.concatenate([-w[..., half:], w[..., :half]], axis=-1)


def _rope_gain_table(seq, gain_rope):
    cos, sin = _rope_angles(seq, MLA_ROPE)
    base = jnp.concatenate([cos, cos, sin, sin], axis=-1)
    half = MLA_ROPE // 2
    gains = jnp.concatenate([gain_rope, gain_rope[half:], gain_rope[:half]])
    return base * gains[None, :]


def _mla_proj(x2d, gain, w_in, q_a_gain, kv_a_gain, w_uq, w_ukv, q_gain, k_gain, seq, tm, sub_tiles):
    t, d = x2d.shape
    kr_lo = MLA_Q_RANK + MLA_KV_RANK
    w_in_ext = jnp.concatenate([w_in, _rotate_half_cols(w_in[:, kr_lo:])], axis=1)
    w_uq_h = w_uq.reshape(MLA_Q_RANK, MLA_HEADS, MLA_QKD)
    w_uq_ext = jnp.concatenate([w_uq_h, _rotate_half_cols(w_uq_h[..., MLA_NOPE:])], axis=-1)
    w_uq_ext = w_uq_ext.reshape(MLA_Q_RANK, MLA_HEADS * MLA_HEAD_W)
    q_rope = _rope_gain_table(seq, q_gain[MLA_NOPE:])
    k_rope = _rope_gain_table(seq, k_gain[MLA_NOPE:])
    qk_w = MLA_HEADS * MLA_HEAD_W
    v_w = MLA_HEADS * MLA_VD
    block = sub_tiles * tm
    pos_blocks = seq // block
    row = lambda i: (i, 0)
    pos = lambda i: (i % pos_blocks, 0)
    return pl.pallas_call(
        functools.partial(_mla_proj_kernel, sub_tiles=sub_tiles),
        grid=(t // block,),
        in_specs=[
            pl.BlockSpec((block, d), row),
            _resident((1, d)),
            _resident(w_in_ext.shape),
            _resident((1, MLA_Q_RANK)),
            _resident((1, MLA_KV_RANK)),
            _resident(w_uq_ext.shape),
            _resident(w_ukv.shape),
            _resident((1, MLA_NOPE)),
            _resident((1, MLA_NOPE)),
            pl.BlockSpec((block, LANES), pos),
            pl.BlockSpec((block, LANES), pos),
        ],
        out_specs=[
            pl.BlockSpec((block, qk_w), row),
            pl.BlockSpec((block, qk_w), row),
            pl.BlockSpec((block, v_w), row),
        ],
        out_shape=[
            jax.ShapeDtypeStruct((t, qk_w), MXU_DTYPE),
            jax.ShapeDtypeStruct((t, qk_w), MXU_DTYPE),
            jax.ShapeDtypeStruct((t, v_w), MXU_DTYPE),
        ],
        compiler_params=_params("parallel"),
        name="mla_proj",
    )(x2d, gain.reshape(1, d), w_in_ext.astype(MXU_DTYPE), q_a_gain.reshape(1, -1), kv_a_gain.reshape(1, -1),
      w_uq_ext.astype(MXU_DTYPE), w_ukv.astype(MXU_DTYPE), q_gain[:MLA_NOPE].reshape(1, -1),
      k_gain[:MLA_NOPE].reshape(1, -1), q_rope, k_rope)


def _attn_kernel(q_ref, k_ref, v_ref, o_ref, m_ref, acc_ref, *, block):
    n_blocks = q_ref.shape[1] // block
    half = block // 2

    units = []
    for qb in range(n_blocks):
        q0 = qb * block
        units += [((q0, block), (kb * block, block)) for kb in range(qb)]
        units += [((q0, half), (q0, half)), ((q0 + half, half), (q0, block))]

    def span(start_size):
        return slice(start_size[0], start_size[0] + start_size[1])

    def scores(q_rows, k_rows):
        return lax.dot_general(q_ref[0, span(q_rows), :], k_ref[0, span(k_rows), :], _NT,
                               preferred_element_type=F32)

    def update(s, q_rows, k_rows):
        (q0, nq), (k0, nk) = q_rows, k_rows
        slot = (q0 // block) % 2
        local = slice(q0 % block, q0 % block + nq)
        first = k0 == 0
        last = k0 + nk == q0 + nq
        if last:
            q_chunk = (q0 + lax.broadcasted_iota(jnp.int32, s.shape, 0)) // CHUNK
            k_chunk = (k0 + lax.broadcasted_iota(jnp.int32, s.shape, 1)) // CHUNK
            s = jnp.where(k_chunk <= q_chunk, s, -1e30)
        m_new = jnp.broadcast_to(jnp.max(s, axis=-1, keepdims=True), (nq, LANES))
        if not first:
            m_prev = m_ref[slot, local, :]
            m_new = jnp.maximum(m_prev, m_new)
        p = jnp.exp2(s - jnp.tile(m_new, (1, nk // LANES)))
        ones = jnp.ones((nk, LANES), MXU_DTYPE)
        acc = _dot(p.astype(MXU_DTYPE), jnp.concatenate([v_ref[0, span(k_rows), :], ones], axis=1))
        if not first:
            acc = jnp.tile(jnp.exp2(m_prev - m_new), (1, 2)) * acc_ref[slot, local, :] + acc
        if last:
            o_ref[0, span(q_rows), :] = (acc[:, :MLA_VD] / acc[:, MLA_VD:]).astype(o_ref.dtype)
        else:
            m_ref[slot, local, :] = m_new
            acc_ref[slot, local, :] = acc

    s_next = scores(*units[0])
    for u, unit in enumerate(units):
        s = s_next
        if u + 1 < len(units):
            s_next = scores(*units[u + 1])
        update(s, *unit)


def _attention(q, k, v, block):
    b, s, _ = q.shape
    return pl.pallas_call(
        functools.partial(_attn_kernel, block=block),
        grid=(b, MLA_HEADS),
        in_specs=[
            pl.BlockSpec((1, s, MLA_HEAD_W), lambda i, h: (i, 0, h)),
            pl.BlockSpec((1, s, MLA_HEAD_W), lambda i, h: (i, 0, h)),
            pl.BlockSpec((1, s, MLA_VD), lambda i, h: (i, 0, h)),
        ],
        out_specs=pl.BlockSpec((1, s, MLA_VD), lambda i, h: (i, 0, h)),
        out_shape=jax.ShapeDtypeStruct((b, s, MLA_HEADS * MLA_VD), MXU_DTYPE),
        scratch_shapes=[
            pltpu.VMEM((2, block, LANES), F32),
            pltpu.VMEM((2, block, MLA_VD + LANES), F32),
        ],
        compiler_params=_params("parallel", "parallel"),
        name="mla_attention",
    )(q, k, v)


def _tiles(seq):
    assert seq % RET_SUPER == 0, seq
    tm = 512 if seq % 512 == 0 else RET_SUPER
    mla_sub_tiles = 2 if seq % (2 * tm) == 0 else 1
    ret_block = 2 * tm if seq % (2 * tm) == 0 else tm
    return dict(tm=tm, ret_block=ret_block, attn_block=tm, ff_chunk=512, mla_sub_tiles=mla_sub_tiles)


def kernel(x, p, mix_norm, ret_w_in, ret_gn, ret_w_out, mla_w_in, mla_q_a_norm, mla_kv_a_norm, mla_w_uq,
           mla_w_ukv, mla_q_norm, mla_k_norm, mla_w_out, mlp_norm, mlp_w1, mlp_w2, ple_norm, ple_gate_w,
           ple_proj_w):
    b, s, d = x.shape
    depth = p.shape[0]
    t = b * s
    cfg = _tiles(s)
    tm = cfg["tm"]
    h = x.reshape(t, d)
    p3d = p.reshape(depth, t, -1)
    mlp_gain = mlp_norm.reshape(depth, 1, d)
    ple_gain = ple_norm.reshape(depth, 1, d)
    w1, w2, wg, wp = (w.astype(MXU_DTYPE) for w in (mlp_w1, mlp_w2, ple_gate_w, ple_proj_w))
    for i in range(depth):
        j = i // 2
        if i % 2 == 0:
            q, k, v, gate = _ret_proj(h, mix_norm[i], ret_w_in[j], ret_gn[j], s, cfg["ret_block"])
            y = _retention(q.reshape(b, s, -1), k.reshape(b, s, -1), v.reshape(b, s, -1), cfg["ret_block"])
            w_out = ret_w_out[j]
        else:
            q, k, v = _mla_proj(h, mix_norm[i], mla_w_in[j], mla_q_a_norm[j], mla_kv_a_norm[j], mla_w_uq[j],
                                mla_w_ukv[j], mla_q_norm[j], mla_k_norm[j], s, tm, cfg["mla_sub_tiles"])
            gate = None
            y = _attention(q.reshape(b, s, -1), k.reshape(b, s, -1), v.reshape(b, s, -1), cfg["attn_block"])
            w_out = mla_w_out[j]
        h = _tail(h, y.reshape(t, -1), gate, p3d, w_out, mlp_gain, w1, w2, ple_gain, wg, wp, i, tm,
                  cfg["ff_chunk"])
    return h.reshape(b, s, d)
```

```python
import functools

import jax
import jax.numpy as jnp
import numpy as np
from jax import lax
from jax.experimental import pallas as pl
from jax.experimental.pallas import tpu as pltpu

EPS = 1e-6
ROPE_THETA = 10000.0
CHUNK = 64
LANES = 128
MXU_WIDTH = 256

RET_HEADS = 4
RET_DK = 256
RET_DV = 512
RET_SUPER = 256

MLA_HEADS = 8
MLA_NOPE = 128
MLA_ROPE = 64
MLA_QKD = MLA_NOPE + MLA_ROPE
MLA_VD = 128
MLA_Q_RANK = 384
MLA_KV_RANK = 256
MLA_HEAD_W = 256

MXU_DTYPE = jnp.bfloat16
F32 = jnp.float32
LOG2_E = 1.4426950408889634
VMEM_LIMIT = 56 * 1024 * 1024

_NT = (((1,), (1,)), ((), ()))
_TN = (((0,), (0,)), ((), ()))


def _dot(a, b):
    return jnp.dot(a, b, preferred_element_type=F32)


def _rms_scale(x):
    return lax.rsqrt(jnp.mean(x * x, axis=-1, keepdims=True) + EPS)


def _rms_scale_lanes(x):
    return jnp.broadcast_to(_rms_scale(x), (x.shape[0], LANES))


def _row_sumsq(x, weights=None):
    if weights is None:
        weights = jnp.ones((x.shape[-1], LANES), MXU_DTYPE)
    return _dot((x * x).astype(MXU_DTYPE), weights)


def _lanes(a, width):
    return a if width == LANES else jnp.tile(a, (1, width // LANES))


def _sigmoid(x):
    return 0.5 * jnp.tanh(0.5 * x) + 0.5


def _resident(shape):
    zeros = (0,) * len(shape)
    return pl.BlockSpec(shape, lambda *_: zeros, pipeline_mode=pl.Buffered(1))


def _params(*semantics):
    return pltpu.CompilerParams(dimension_semantics=semantics, vmem_limit_bytes=VMEM_LIMIT)


def _rope_angles(seq, dim):
    inv = 1.0 / (ROPE_THETA ** (jnp.arange(0, dim, 2, dtype=F32) / dim))
    ang = jnp.arange(seq, dtype=F32)[:, None] * inv[None, :]
    return jnp.cos(ang), jnp.sin(ang)


def _ret_proj_kernel(x_ref, g_ref, w_ref, cos_ref, sin_ref, gn_ref, q_ref, k_ref, v_ref, gate_ref):
    x = x_ref[...]
    inv = _rms_scale_lanes(x)
    xg = (x * g_ref[...]).astype(MXU_DTYPE)
    inv_wide = _lanes(inv, MXU_WIDTH)
    cos = cos_ref[...] * inv
    sin = sin_ref[...] * inv
    half = RET_DK // 2
    qk_w = RET_HEADS * RET_DK
    v_w = RET_HEADS * RET_DV
    k_scale = RET_DK ** -0.5
    for lo in range(0, v_w, MXU_WIDTH):
        g = _dot(xg, w_ref[:, 2 * qk_w + v_w + lo:2 * qk_w + v_w + lo + MXU_WIDTH]) * inv_wide
        gate_ref[:, lo:lo + MXU_WIDTH] = (g * _sigmoid(g) * gn_ref[:, lo:lo + MXU_WIDTH]).astype(gate_ref.dtype)
        v_ref[:, lo:lo + MXU_WIDTH] = (
            _dot(xg, w_ref[:, 2 * qk_w + lo:2 * qk_w + lo + MXU_WIDTH]) * inv_wide).astype(v_ref.dtype)
    for h in range(RET_HEADS):
        lo = h * RET_DK
        r = _dot(xg, w_ref[:, lo:lo + RET_DK])
        x1, x2 = r[:, :half], r[:, half:]
        q_ref[:, lo:lo + half] = (x1 * cos - x2 * sin).astype(q_ref.dtype)
        q_ref[:, lo + half:lo + RET_DK] = (x2 * cos + x1 * sin).astype(q_ref.dtype)
        r = _dot(xg, w_ref[:, qk_w + lo:qk_w + lo + RET_DK])
        x1, x2 = r[:, :half], r[:, half:]
        k_ref[:, lo:lo + half] = ((x1 * cos - x2 * sin) * k_scale).astype(k_ref.dtype)
        k_ref[:, lo + half:lo + RET_DK] = ((x2 * cos + x1 * sin) * k_scale).astype(k_ref.dtype)


def _ret_proj(x2d, gain, w_in, gn_gain, seq, tm):
    t, d = x2d.shape
    qk_w = RET_HEADS * RET_DK
    v_w = RET_HEADS * RET_DV
    cos, sin = _rope_angles(seq, RET_DK)
    pos_blocks = seq // tm
    row = lambda i: (i, 0)
    pos = lambda i: (i % pos_blocks, 0)
    return pl.pallas_call(
        _ret_proj_kernel,
        grid=(t // tm,),
        in_specs=[
            pl.BlockSpec((tm, d), row),
            _resident((1, d)),
            _resident(w_in.shape),
            pl.BlockSpec((tm, RET_DK // 2), pos),
            pl.BlockSpec((tm, RET_DK // 2), pos),
            _resident((1, v_w)),
        ],
        out_specs=[
            pl.BlockSpec((tm, qk_w), row),
            pl.BlockSpec((tm, qk_w), row),
            pl.BlockSpec((tm, v_w), row),
            pl.BlockSpec((tm, v_w), row),
        ],
        out_shape=[
            jax.ShapeDtypeStruct((t, qk_w), MXU_DTYPE),
            jax.ShapeDtypeStruct((t, qk_w), MXU_DTYPE),
            jax.ShapeDtypeStruct((t, v_w), MXU_DTYPE),
            jax.ShapeDtypeStruct((t, v_w), MXU_DTYPE),
        ],
        compiler_params=_params("parallel"),
        name="ret_proj",
    )(x2d, gain.reshape(1, d), w_in.astype(MXU_DTYPE), cos, sin, gn_gain.reshape(1, v_w))


def _retention_kernel(q_ref, k_ref, v_ref, decay_ref, qd_ref, kd_ref, cd_ref, y_ref, state_ref, *, groups):
    @pl.when(pl.program_id(1) == 0)
    def _():
        state_ref[...] = jnp.zeros_like(state_ref)

    def matmuls(c, h):
        rows = slice(c * RET_SUPER, (c + 1) * RET_SUPER)
        qk_cols = slice(h * RET_DK, (h + 1) * RET_DK)
        v_cols = slice(h * RET_DV, (h + 1) * RET_DV)
        q = q_ref[0, rows, qk_cols]
        k = k_ref[0, rows, qk_cols]
        v = v_ref[0, rows, v_cols]
        state = state_ref[h]
        scores = lax.dot_general(q, k, _NT, preferred_element_type=F32) * decay_ref[h]
        lhs = jnp.concatenate([q * qd_ref[h], scores.astype(MXU_DTYPE)], axis=1)
        rhs = jnp.concatenate([state.astype(MXU_DTYPE), v], axis=0)
        out = _dot(lhs, rhs)
        state_ref[h] = state * cd_ref[h] + lax.dot_general(k * kd_ref[h], v, _TN, preferred_element_type=F32)
        return out

    def finish(c, h, out):
        rows = slice(c * RET_SUPER, (c + 1) * RET_SUPER)
        v_cols = slice(h * RET_DV, (h + 1) * RET_DV)
        y_ref[0, rows, v_cols] = out.astype(y_ref.dtype)

    units = [(c, h) for c in range(groups) for h in range(RET_HEADS)]
    ahead = matmuls(*units[0])
    for u, unit in enumerate(units):
        out = ahead
        if u + 1 < len(units):
            ahead = matmuls(*units[u + 1])
        finish(*unit, out)


def _retention_tables():
    log_gamma = np.log(1.0 - 2.0 ** (-5.0 - np.arange(RET_HEADS, dtype=np.float64)))
    idx = np.arange(RET_SUPER, dtype=np.float64)
    dist = np.abs(idx[:, None] - idx[None, :])
    chunk = np.arange(RET_SUPER) // CHUNK
    visible = chunk[None, :] <= chunk[:, None]
    decay = np.where(visible[None], np.exp(log_gamma[:, None, None] * dist), 0.0)
    q_decay = np.exp(log_gamma[:, None] * (idx + 1.0))
    k_decay = np.exp(log_gamma[:, None] * (RET_SUPER - 1.0 - idx))
    group_decay = np.exp(log_gamma * RET_SUPER)
    qd = np.broadcast_to(q_decay[:, :, None], (RET_HEADS, RET_SUPER, RET_DK))
    kd = np.broadcast_to(k_decay[:, :, None], (RET_HEADS, RET_SUPER, RET_DK))
    cd = np.broadcast_to(group_decay[:, None, None], (RET_HEADS, 1, RET_DV))
    decay, qd, kd, cd = (jnp.asarray(np.ascontiguousarray(a), F32) for a in (decay, qd, kd, cd))
    return decay, qd.astype(MXU_DTYPE), kd.astype(MXU_DTYPE), cd


def _retention(q, k, v, block):
    b, s, _ = q.shape
    qk_w = RET_HEADS * RET_DK
    v_w = RET_HEADS * RET_DV
    decay, qd, kd, cd = _retention_tables()
    tok = lambda i, j: (i, j, 0)
    return pl.pallas_call(
        functools.partial(_retention_kernel, groups=block // RET_SUPER),
        grid=(b, s // block),
        in_specs=[
            pl.BlockSpec((1, block, qk_w), tok),
            pl.BlockSpec((1, block, qk_w), tok),
            pl.BlockSpec((1, block, v_w), tok),
            _resident(decay.shape),
            _resident(qd.shape),
            _resident(kd.shape),
            _resident(cd.shape),
        ],
        out_specs=pl.BlockSpec((1, block, v_w), tok),
        out_shape=jax.ShapeDtypeStruct((b, s, v_w), MXU_DTYPE),
        scratch_shapes=[pltpu.VMEM((RET_HEADS, RET_DK, RET_DV), F32)],
        compiler_params=_params("parallel", "arbitrary"),
        name="retention",
    )(q, k, v, decay, qd, kd, cd)


def _tail_kernel(*refs, ff_chunk, gated_heads):
    if gated_heads:
        h_ref, y_ref, gate_ref, p_ref, wo_ref, mg_ref, w1_ref, w2_ref, pg_ref, wg_ref, wp_ref, o_ref = refs
    else:
        h_ref, y_ref, p_ref, wo_ref, mg_ref, w1_ref, w2_ref, pg_ref, wg_ref, wp_ref, o_ref = refs
    h = h_ref[...]
    d = h.shape[-1]
    if gated_heads:
        width = y_ref.shape[1] // gated_heads
        for head in range(gated_heads):
            cols = slice(head * width, (head + 1) * width)
            y = y_ref[:, cols].astype(F32)
            normed = (y * _rms_scale(y)).astype(MXU_DTYPE)
            h = h + _dot(gate_ref[:, cols] * normed, wo_ref[cols, :])
    else:
        h = h + _dot(y_ref[...], wo_ref[...])
    inv = _lanes(_rms_scale_lanes(h), ff_chunk)
    hg = (h * mg_ref[...]).astype(MXU_DTYPE)
    d_ff = w1_ref.shape[1]
    mlp = None
    for c in range(d_ff // ff_chunk):
        cols = slice(c * ff_chunk, (c + 1) * ff_chunk)
        a = jnp.square(jnp.maximum(_dot(hg, w1_ref[:, cols]) * inv, 0.0))
        part = _dot(a.astype(MXU_DTYPE), w2_ref[cols, :])
        mlp = part if mlp is None else mlp + part
    h = h + mlp
    inv = _lanes(_rms_scale_lanes(h), MXU_WIDTH)
    hg = (h * pg_ref[...]).astype(MXU_DTYPE)
    pe = p_ref[...].astype(MXU_DTYPE)
    for lo in range(0, d, MXU_WIDTH):
        cols = slice(lo, lo + MXU_WIDTH)
        gate = _sigmoid(_dot(hg, wg_ref[:, cols]) * inv)
        o_ref[:, cols] = h[:, cols] + gate * _dot(pe, wp_ref[:, cols])


def _layer_of(shape, layer):
    index = (layer,) + (0,) * (len(shape) - 1)
    return pl.BlockSpec((None,) + tuple(shape[1:]), lambda *_: index, pipeline_mode=pl.Buffered(1))


def _tail(h2d, y2d, gate2d, p3d, w_out, mlp_gain, w1, w2, ple_gain, wg, wp, layer, tm, ff_chunk):
    t, d = h2d.shape
    row = lambda i: (i, 0)
    mixer = [y2d] if gate2d is None else [y2d, gate2d]
    return pl.pallas_call(
        functools.partial(_tail_kernel, ff_chunk=ff_chunk, gated_heads=0 if gate2d is None else RET_HEADS),
        grid=(t // tm,),
        in_specs=[
            pl.BlockSpec((tm, d), row),
            *[pl.BlockSpec((tm, a.shape[1]), row) for a in mixer],
            pl.BlockSpec((None, tm, p3d.shape[2]), lambda i: (layer, i, 0)),
            _resident(w_out.shape),
            _layer_of(mlp_gain.shape, layer),
            _layer_of(w1.shape, layer),
            _layer_of(w2.shape, layer),
            _layer_of(ple_gain.shape, layer),
            _layer_of(wg.shape, layer),
            _layer_of(wp.shape, layer),
        ],
        out_specs=pl.BlockSpec((tm, d), row),
        out_shape=jax.ShapeDtypeStruct((t, d), F32),
        compiler_params=_params("parallel"),
        name="layer_tail",
    )(h2d, *mixer, p3d, w_out.astype(MXU_DTYPE), mlp_gain, w1, w2, ple_gain, wg, wp)


def _mla_proj_kernel(x_ref, g_ref, win_ref, qa_ref, kva_ref, wuq_ref, wukv_ref, qn_ref, kn_ref,
                     qrope_ref, krope_ref, q_ref, k_ref, v_ref, *, sub_tiles):
    tm = x_ref.shape[0] // sub_tiles
    d = x_ref.shape[-1]
    lane = lax.broadcasted_iota(jnp.int32, (1, LANES), 1)
    first_half = (lane < MLA_ROPE).astype(F32)
    softmax_scale = MLA_QKD ** -0.5 * LOG2_E
    qn = qn_ref[...]
    kn = kn_ref[...]
    ones = jnp.ones((MLA_NOPE, LANES), MXU_DTYPE)
    halves = jnp.full((LANES, LANES), 0.5, MXU_DTYPE)
    head_weights = jnp.concatenate([ones, halves], axis=0)

    def down(rows):
        x = x_ref[rows, :]
        inv = lax.rsqrt(_row_sumsq(x) * (1.0 / d) + EPS)
        proj = _dot((x * g_ref[...]).astype(MXU_DTYPE), win_ref[...])
        proj = proj * _lanes(inv, proj.shape[-1])
        c_q = proj[:, :MLA_Q_RANK]
        c_kv = proj[:, MLA_Q_RANK:MLA_Q_RANK + MLA_KV_RANK]
        k_r = proj[:, MLA_Q_RANK + MLA_KV_RANK:]
        inv = lax.rsqrt(_row_sumsq(c_q) * (1.0 / MLA_Q_RANK) + EPS)
        c_q = (c_q * _lanes(inv, MLA_Q_RANK) * qa_ref[...]).astype(MXU_DTYPE)
        inv = lax.rsqrt(_row_sumsq(c_kv) * (1.0 / MLA_KV_RANK) + EPS)
        c_kv = (c_kv * _lanes(inv, MLA_KV_RANK) * kva_ref[...]).astype(MXU_DTYPE)
        k_rot = k_r * krope_ref[rows, :]
        k_rot = (k_rot + pltpu.roll(k_rot, MLA_ROPE, 1)) * first_half
        return c_q, c_kv, k_rot, _row_sumsq(k_r, halves)

    tiles = [slice(i * tm, (i + 1) * tm) for i in range(sub_tiles)]
    projected = [down(rows) for rows in tiles]

    def up_project(t, h):
        c_q, c_kv = projected[t][:2]
        cols = slice(h * MLA_HEAD_W, (h + 1) * MLA_HEAD_W)
        return _dot(c_q, wuq_ref[:, cols]), _dot(c_kv, wukv_ref[:, cols])

    def normalise(t, h, qh, kvh):
        rows = tiles[t]
        k_rot, k_r_ss = projected[t][2:]
        lo = h * MLA_HEAD_W
        q_nope, q_r = qh[:, :MLA_NOPE], qh[:, MLA_NOPE:]
        inv = lax.rsqrt(_row_sumsq(qh, head_weights) * (1.0 / MLA_QKD) + EPS) * softmax_scale
        q_rot = q_r * qrope_ref[rows, :]
        q_rot = q_rot + pltpu.roll(q_rot, MLA_ROPE, 1)
        q_ref[rows, lo:lo + MLA_NOPE] = (q_nope * inv * qn).astype(q_ref.dtype)
        q_ref[rows, lo + MLA_NOPE:lo + MLA_HEAD_W] = (q_rot * inv).astype(q_ref.dtype)

        k_nope, v = kvh[:, :MLA_NOPE], kvh[:, MLA_NOPE:]
        inv = lax.rsqrt((_row_sumsq(k_nope, ones) + k_r_ss) * (1.0 / MLA_QKD) + EPS)
        k_ref[rows, lo:lo + MLA_NOPE] = (k_nope * inv * kn).astype(k_ref.dtype)
        k_ref[rows, lo + MLA_NOPE:lo + MLA_HEAD_W] = (k_rot * inv).astype(k_ref.dtype)
        v_ref[rows, h * MLA_VD:(h + 1) * MLA_VD] = v.astype(v_ref.dtype)

    units = [(t, h) for t in range(sub_tiles) for h in range(MLA_HEADS)]
    ahead = up_project(*units[0])
    for u, unit in enumerate(units):
        current = ahead
        if u + 1 < len(units):
            ahead = up_project(*units[u + 1])
        normalise(*unit, *current)


def _rotate_half_cols(w):
    half = w.shape[-1] // 2
    return jnp.concatenate([-w[..., half:], w[..., :half]], axis=-1)


def _rope_gain_table(seq, gain_rope):
    cos, sin = _rope_angles(seq, MLA_ROPE)
    base = jnp.concatenate([cos, cos, sin, sin], axis=-1)
    half = MLA_ROPE // 2
    gains = jnp.concatenate([gain_rope, gain_rope[half:], gain_rope[:half]])
    return base * gains[None, :]


def _mla_proj(x2d, gain, w_in, q_a_gain, kv_a_gain, w_uq, w_ukv, q_gain, k_gain, seq, tm, sub_tiles):
    t, d = x2d.shape
    kr_lo = MLA_Q_RANK + MLA_KV_RANK
    w_in_ext = jnp.concatenate([w_in, _rotate_half_cols(w_in[:, kr_lo:])], axis=1)
    w_uq_h = w_uq.reshape(MLA_Q_RANK, MLA_HEADS, MLA_QKD)
    w_uq_ext = jnp.concatenate([w_uq_h, _rotate_half_cols(w_uq_h[..., MLA_NOPE:])], axis=-1)
    w_uq_ext = w_uq_ext.reshape(MLA_Q_RANK, MLA_HEADS * MLA_HEAD_W)
    q_rope = _rope_gain_table(seq, q_gain[MLA_NOPE:])
    k_rope = _rope_gain_table(seq, k_gain[MLA_NOPE:])
    qk_w = MLA_HEADS * MLA_HEAD_W
    v_w = MLA_HEADS * MLA_VD
    block = sub_tiles * tm
    pos_blocks = seq // block
    row = lambda i: (i, 0)
    pos = lambda i: (i % pos_blocks, 0)
    return pl.pallas_call(
        functools.partial(_mla_proj_kernel, sub_tiles=sub_tiles),
        grid=(t // block,),
        in_specs=[
            pl.BlockSpec((block, d), row),
            _resident((1, d)),
            _resident(w_in_ext.shape),
            _resident((1, MLA_Q_RANK)),
            _resident((1, MLA_KV_RANK)),
            _resident(w_uq_ext.shape),
            _resident(w_ukv.shape),
            _resident((1, MLA_NOPE)),
            _resident((1, MLA_NOPE)),
            pl.BlockSpec((block, LANES), pos),
            pl.BlockSpec((block, LANES), pos),
        ],
        out_specs=[
            pl.BlockSpec((block, qk_w), row),
            pl.BlockSpec((block, qk_w), row),
            pl.BlockSpec((block, v_w), row),
        ],
        out_shape=[
            jax.ShapeDtypeStruct((t, qk_w), MXU_DTYPE),
            jax.ShapeDtypeStruct((t, qk_w), MXU_DTYPE),
            jax.ShapeDtypeStruct((t, v_w), MXU_DTYPE),
        ],
        compiler_params=_params("parallel"),
        name="mla_proj",
    )(x2d, gain.reshape(1, d), w_in_ext.astype(MXU_DTYPE), q_a_gain.reshape(1, -1), kv_a_gain.reshape(1, -1),
      w_uq_ext.astype(MXU_DTYPE), w_ukv.astype(MXU_DTYPE), q_gain[:MLA_NOPE].reshape(1, -1),
      k_gain[:MLA_NOPE].reshape(1, -1), q_rope, k_rope)


def _attn_kernel(q_ref, k_ref, v_ref, o_ref, m_ref, acc_ref, *, block):
    n_blocks = q_ref.shape[1] // block
    half = block // 2

    units = []
    for qb in range(n_blocks):
        q0 = qb * block
        units += [((q0, block), (kb * block, block)) for kb in range(qb)]
        units += [((q0, half), (q0, half)), ((q0 + half, half), (q0, block))]

    def span(start_size):
        return slice(start_size[0], start_size[0] + start_size[1])

    def scores(q_rows, k_rows):
        return lax.dot_general(q_ref[0, span(q_rows), :], k_ref[0, span(k_rows), :], _NT,
                               preferred_element_type=F32)

    def update(s, q_rows, k_rows):
        (q0, nq), (k0, nk) = q_rows, k_rows
        slot = (q0 // block) % 2
        local = slice(q0 % block, q0 % block + nq)
        first = k0 == 0
        last = k0 + nk == q0 + nq
        if last:
            q_chunk = (q0 + lax.broadcasted_iota(jnp.int32, s.shape, 0)) // CHUNK
            k_chunk = (k0 + lax.broadcasted_iota(jnp.int32, s.shape, 1)) // CHUNK
            s = jnp.where(k_chunk <= q_chunk, s, -1e30)
        m_new = jnp.broadcast_to(jnp.max(s, axis=-1, keepdims=True), (nq, LANES))
        if not first:
            m_prev = m_ref[slot, local, :]
            m_new = jnp.maximum(m_prev, m_new)
        p = jnp.exp2(s - jnp.tile(m_new, (1, nk // LANES)))
        ones = jnp.ones((nk, LANES), MXU_DTYPE)
        acc = _dot(p.astype(MXU_DTYPE), jnp.concatenate([v_ref[0, span(k_rows), :], ones], axis=1))
        if not first:
            acc = jnp.tile(jnp.exp2(m_prev - m_new), (1, 2)) * acc_ref[slot, local, :] + acc
        if last:
            o_ref[0, span(q_rows), :] = (acc[:, :MLA_VD] / acc[:, MLA_VD:]).astype(o_ref.dtype)
        else:
            m_ref[slot, local, :] = m_new
            acc_ref[slot, local, :] = acc

    s_next = scores(*units[0])
    for u, unit in enumerate(units):
        s = s_next
        if u + 1 < len(units):
            s_next = scores(*units[u + 1])
        update(s, *unit)


def _attention(q, k, v, block):
    b, s, _ = q.shape
    return pl.pallas_call(
        functools.partial(_attn_kernel, block=block),
        grid=(b, MLA_HEADS),
        in_specs=[
            pl.BlockSpec((1, s, MLA_HEAD_W), lambda i, h: (i, 0, h)),
            pl.BlockSpec((1, s, MLA_HEAD_W), lambda i, h: (i, 0, h)),
            pl.BlockSpec((1, s, MLA_VD), lambda i, h: (i, 0, h)),
        ],
        out_specs=pl.BlockSpec((1, s, MLA_VD), lambda i, h: (i, 0, h)),
        out_shape=jax.ShapeDtypeStruct((b, s, MLA_HEADS * MLA_VD), MXU_DTYPE),
        scratch_shapes=[
            pltpu.VMEM((2, block, LANES), F32),
            pltpu.VMEM((2, block, MLA_VD + LANES), F32),
        ],
        compiler_params=_params("parallel", "parallel"),
        name="mla_attention",
    )(q, k, v)


def _tiles(seq):
    assert seq % RET_SUPER == 0, seq
    tm = 512 if seq % 512 == 0 else RET_SUPER
    mla_sub_tiles = 2 if seq % (2 * tm) == 0 else 1
    ret_block = 2 * tm if seq % (2 * tm) == 0 else tm
    return dict(tm=tm, ret_block=ret_block, attn_block=tm, ff_chunk=512, mla_sub_tiles=mla_sub_tiles)


def kernel(x, p, mix_norm, ret_w_in, ret_gn, ret_w_out, mla_w_in, mla_q_a_norm, mla_kv_a_norm, mla_w_uq,
           mla_w_ukv, mla_q_norm, mla_k_norm, mla_w_out, mlp_norm, mlp_w1, mlp_w2, ple_norm, ple_gate_w,
           ple_proj_w):
    b, s, d = x.shape
    depth = p.shape[0]
    t = b * s
    cfg = _tiles(s)
    tm = cfg["tm"]
    h = x.reshape(t, d)
    p3d = p.reshape(depth, t, -1)
    mlp_gain = mlp_norm.reshape(depth, 1, d)
    ple_gain = ple_norm.reshape(depth, 1, d)
    w1, w2, wg, wp = (w.astype(MXU_DTYPE) for w in (mlp_w1, mlp_w2, ple_gate_w, ple_proj_w))
    for i in range(depth):
        j = i // 2
        if i % 2 == 0:
            q, k, v, gate = _ret_proj(h, mix_norm[i], ret_w_in[j], ret_gn[j], s, cfg["ret_block"])
            y = _retention(q.reshape(b, s, -1), k.reshape(b, s, -1), v.reshape(b, s, -1), cfg["ret_block"])
            w_out = ret_w_out[j]
        else:
            q, k, v = _mla_proj(h, mix_norm[i], mla_w_in[j], mla_q_a_norm[j], mla_kv_a_norm[j], mla_w_uq[j],
                                mla_w_ukv[j], mla_q_norm[j], mla_k_norm[j], s, tm, cfg["mla_sub_tiles"])
            gate = None
            y = _attention(q.reshape(b, s, -1), k.reshape(b, s, -1), v.reshape(b, s, -1), cfg["attn_block"])
            w_out = mla_w_out[j]
        h = _tail(h, y.reshape(t, -1), gate, p3d, w_out, mlp_gain, w1, w2, ple_gain, wg, wp, i, tm,
                  cfg["ff_chunk"])
    return h.reshape(b, s, d)
```

```python
import functools

import jax
import jax.numpy as jnp
import numpy as np
from jax import lax
from jax.experimental import pallas as pl
from jax.experimental.pallas import tpu as pltpu

EPS = 1e-6
ROPE_THETA = 10000.0
CHUNK = 64
LANES = 128
MXU_WIDTH = 256

RET_HEADS = 4
RET_DK = 256
RET_DV = 512
RET_SUPER = 256

MLA_HEADS = 8
MLA_NOPE = 128
MLA_ROPE = 64
MLA_QKD = MLA_NOPE + MLA_ROPE
MLA_VD = 128
MLA_Q_RANK = 384
MLA_KV_RANK = 256
MLA_HEAD_W = 256

MXU_DTYPE = jnp.bfloat16
F32 = jnp.float32
LOG2_E = 1.4426950408889634
VMEM_LIMIT = 56 * 1024 * 1024

_NT = (((1,), (1,)), ((), ()))
_TN = (((0,), (0,)), ((), ()))


def _dot(a, b):
    return jnp.dot(a, b, preferred_element_type=F32)


def _rms_scale(x):
    return lax.rsqrt(jnp.mean(x * x, axis=-1, keepdims=True) + EPS)


def _rms_scale_lanes(x):
    return jnp.broadcast_to(_rms_scale(x), (x.shape[0], LANES))


def _row_sumsq(x, weights=None):
    if weights is None:
        weights = jnp.ones((x.shape[-1], LANES), MXU_DTYPE)
    return _dot((x * x).astype(MXU_DTYPE), weights)


def _lanes(a, width):
    return a if width == LANES else jnp.tile(a, (1, width // LANES))


def _sigmoid(x):
    return 0.5 * jnp.tanh(0.5 * x) + 0.5


def _resident(shape):
    zeros = (0,) * len(shape)
    return pl.BlockSpec(shape, lambda *_: zeros, pipeline_mode=pl.Buffered(1))


def _params(*semantics):
    return pltpu.CompilerParams(dimension_semantics=semantics, vmem_limit_bytes=VMEM_LIMIT)


def _rope_angles(seq, dim):
    inv = 1.0 / (ROPE_THETA ** (jnp.arange(0, dim, 2, dtype=F32) / dim))
    ang = jnp.arange(seq, dtype=F32)[:, None] * inv[None, :]
    return jnp.cos(ang), jnp.sin(ang)


def _ret_proj_kernel(x_ref, g_ref, w_ref, cos_ref, sin_ref, gn_ref, q_ref, k_ref, v_ref, gate_ref):
    x = x_ref[...]
    inv = _rms_scale_lanes(x)
    xg = (x * g_ref[...]).astype(MXU_DTYPE)
    inv_wide = _lanes(inv, MXU_WIDTH)
    cos = cos_ref[...] * inv
    sin = sin_ref[...] * inv
    half = RET_DK // 2
    qk_w = RET_HEADS * RET_DK
    v_w = RET_HEADS * RET_DV
    k_scale = RET_DK ** -0.5
    for lo in range(0, v_w, MXU_WIDTH):
        g = _dot(xg, w_ref[:, 2 * qk_w + v_w + lo:2 * qk_w + v_w + lo + MXU_WIDTH]) * inv_wide
        gate_ref[:, lo:lo + MXU_WIDTH] = (g * _sigmoid(g) * gn_ref[:, lo:lo + MXU_WIDTH]).astype(gate_ref.dtype)
        v_ref[:, lo:lo + MXU_WIDTH] = (
            _dot(xg, w_ref[:, 2 * qk_w + lo:2 * qk_w + lo + MXU_WIDTH]) * inv_wide).astype(v_ref.dtype)
    for h in range(RET_HEADS):
        lo = h * RET_DK
        r = _dot(xg, w_ref[:, lo:lo + RET_DK])
        x1, x2 = r[:, :half], r[:, half:]
        q_ref[:, lo:lo + half] = (x1 * cos - x2 * sin).astype(q_ref.dtype)
        q_ref[:, lo + half:lo + RET_DK] = (x2 * cos + x1 * sin).astype(q_ref.dtype)
        r = _dot(xg, w_ref[:, qk_w + lo:qk_w + lo + RET_DK])
        x1, x2 = r[:, :half], r[:, half:]
        k_ref[:, lo:lo + half] = ((x1 * cos - x2 * sin) * k_scale).astype(k_ref.dtype)
        k_ref[:, lo + half:lo + RET_DK] = ((x2 * cos + x1 * sin) * k_scale).astype(k_ref.dtype)


def _ret_proj(x2d, gain, w_in, gn_gain, seq, tm):
    t, d = x2d.shape
    qk_w = RET_HEADS * RET_DK
    v_w = RET_HEADS * RET_DV
    cos, sin = _rope_angles(seq, RET_DK)
    pos_blocks = seq // tm
    row = lambda i: (i, 0)
    pos = lambda i: (i % pos_blocks, 0)
    return pl.pallas_call(
        _ret_proj_kernel,
        grid=(t // tm,),
        in_specs=[
            pl.BlockSpec((tm, d), row),
            _resident((1, d)),
            _resident(w_in.shape),
            pl.BlockSpec((tm, RET_DK // 2), pos),
            pl.BlockSpec((tm, RET_DK // 2), pos),
            _resident((1, v_w)),
        ],
        out_specs=[
            pl.BlockSpec((tm, qk_w), row),
            pl.BlockSpec((tm, qk_w), row),
            pl.BlockSpec((tm, v_w), row),
            pl.BlockSpec((tm, v_w), row),
        ],
        out_shape=[
            jax.ShapeDtypeStruct((t, qk_w), MXU_DTYPE),
            jax.ShapeDtypeStruct((t, qk_w), MXU_DTYPE),
            jax.ShapeDtypeStruct((t, v_w), MXU_DTYPE),
            jax.ShapeDtypeStruct((t, v_w), MXU_DTYPE),
        ],
        compiler_params=_params("parallel"),
        name="ret_proj",
    )(x2d, gain.reshape(1, d), w_in.astype(MXU_DTYPE), cos, sin, gn_gain.reshape(1, v_w))


def _retention_kernel(q_ref, k_ref, v_ref, decay_ref, qd_ref, kd_ref, cd_ref, y_ref, state_ref, *, groups):
    @pl.when(pl.program_id(1) == 0)
    def _():
        state_ref[...] = jnp.zeros_like(state_ref)

    def matmuls(c, h):
        rows = slice(c * RET_SUPER, (c + 1) * RET_SUPER)
        qk_cols = slice(h * RET_DK, (h + 1) * RET_DK)
        v_cols = slice(h * RET_DV, (h + 1) * RET_DV)
        q = q_ref[0, rows, qk_cols]
        k = k_ref[0, rows, qk_cols]
        v = v_ref[0, rows, v_cols]
        state = state_ref[h]
        scores = lax.dot_general(q, k, _NT, preferred_element_type=F32) * decay_ref[h]
        lhs = jnp.concatenate([q * qd_ref[h], scores.astype(MXU_DTYPE)], axis=1)
        rhs = jnp.concatenate([state.astype(MXU_DTYPE), v], axis=0)
        out = _dot(lhs, rhs)
        state_ref[h] = state * cd_ref[h] + lax.dot_general(k * kd_ref[h], v, _TN, preferred_element_type=F32)
        return out

    def finish(c, h, out):
        rows = slice(c * RET_SUPER, (c + 1) * RET_SUPER)
        v_cols = slice(h * RET_DV, (h + 1) * RET_DV)
        y_ref[0, rows, v_cols] = out.astype(y_ref.dtype)

    units = [(c, h) for c in range(groups) for h in range(RET_HEADS)]
    ahead = matmuls(*units[0])
    for u, unit in enumerate(units):
        out = ahead
        if u + 1 < len(units):
            ahead = matmuls(*units[u + 1])
        finish(*unit, out)


def _retention_tables():
    log_gamma = np.log(1.0 - 2.0 ** (-5.0 - np.arange(RET_HEADS, dtype=np.float64)))
    idx = np.arange(RET_SUPER, dtype=np.float64)
    dist = np.abs(idx[:, None] - idx[None, :])
    chunk = np.arange(RET_SUPER) // CHUNK
    visible = chunk[None, :] <= chunk[:, None]
    decay = np.where(visible[None], np.exp(log_gamma[:, None, None] * dist), 0.0)
    q_decay = np.exp(log_gamma[:, None] * (idx + 1.0))
    k_decay = np.exp(log_gamma[:, None] * (RET_SUPER - 1.0 - idx))
    group_decay = np.exp(log_gamma * RET_SUPER)
    qd = np.broadcast_to(q_decay[:, :, None], (RET_HEADS, RET_SUPER, RET_DK))
    kd = np.broadcast_to(k_decay[:, :, None], (RET_HEADS, RET_SUPER, RET_DK))
    cd = np.broadcast_to(group_decay[:, None, None], (RET_HEADS, 1, RET_DV))
    decay, qd, kd, cd = (jnp.asarray(np.ascontiguousarray(a), F32) for a in (decay, qd, kd, cd))
    return decay, qd.astype(MXU_DTYPE), kd.astype(MXU_DTYPE), cd


def _retention(q, k, v, block):
    b, s, _ = q.shape
    qk_w = RET_HEADS * RET_DK
    v_w = RET_HEADS * RET_DV
    decay, qd, kd, cd = _retention_tables()
    tok = lambda i, j: (i, j, 0)
    return pl.pallas_call(
        functools.partial(_retention_kernel, groups=block // RET_SUPER),
        grid=(b, s // block),
        in_specs=[
            pl.BlockSpec((1, block, qk_w), tok),
            pl.BlockSpec((1, block, qk_w), tok),
            pl.BlockSpec((1, block, v_w), tok),
            _resident(decay.shape),
            _resident(qd.shape),
            _resident(kd.shape),
            _resident(cd.shape),
        ],
        out_specs=pl.BlockSpec((1, block, v_w), tok),
        out_shape=jax.ShapeDtypeStruct((b, s, v_w), MXU_DTYPE),
        scratch_shapes=[pltpu.VMEM((RET_HEADS, RET_DK, RET_DV), F32)],
        compiler_params=_params("parallel", "arbitrary"),
        name="retention",
    )(q, k, v, decay, qd, kd, cd)


def _tail_kernel(*refs, ff_chunk, gated_heads):
    if gated_heads:
        h_ref, y_ref, gate_ref, p_ref, wo_ref, mg_ref, w1_ref, w2_ref, pg_ref, wg_ref, wp_ref, o_ref = refs
    else:
        h_ref, y_ref, p_ref, wo_ref, mg_ref, w1_ref, w2_ref, pg_ref, wg_ref, wp_ref, o_ref = refs
    h = h_ref[...]
    d = h.shape[-1]
    if gated_heads:
        width = y_ref.shape[1] // gated_heads
        for head in range(gated_heads):
            cols = slice(head * width, (head + 1) * width)
            y = y_ref[:, cols].astype(F32)
            normed = (y * _rms_scale(y)).astype(MXU_DTYPE)
            h = h + _dot(gate_ref[:, cols] * normed, wo_ref[cols, :])
    else:
        h = h + _dot(y_ref[...], wo_ref[...])
    inv = _lanes(_rms_scale_lanes(h), ff_chunk)
    hg = (h * mg_ref[...]).astype(MXU_DTYPE)
    d_ff = w1_ref.shape[1]
    mlp = None
    for c in range(d_ff // ff_chunk):
        cols = slice(c * ff_chunk, (c + 1) * ff_chunk)
        a = jnp.square(jnp.maximum(_dot(hg, w1_ref[:, cols]) * inv, 0.0))
        part = _dot(a.astype(MXU_DTYPE), w2_ref[cols, :])
        mlp = part if mlp is None else mlp + part
    h = h + mlp
    inv = _lanes(_rms_scale_lanes(h), MXU_WIDTH)
    hg = (h * pg_ref[...]).astype(MXU_DTYPE)
    pe = p_ref[...].astype(MXU_DTYPE)
    for lo in range(0, d, MXU_WIDTH):
        cols = slice(lo, lo + MXU_WIDTH)
        gate = _sigmoid(_dot(hg, wg_ref[:, cols]) * inv)
        o_ref[:, cols] = h[:, cols] + gate * _dot(pe, wp_ref[:, cols])


def _layer_of(shape, layer):
    index = (layer,) + (0,) * (len(shape) - 1)
    return pl.BlockSpec((None,) + tuple(shape[1:]), lambda *_: index, pipeline_mode=pl.Buffered(1))


def _tail(h2d, y2d, gate2d, p3d, w_out, mlp_gain, w1, w2, ple_gain, wg, wp, layer, tm, ff_chunk):
    t, d = h2d.shape
    row = lambda i: (i, 0)
    mixer = [y2d] if gate2d is None else [y2d, gate2d]
    return pl.pallas_call(
        functools.partial(_tail_kernel, ff_chunk=ff_chunk, gated_heads=0 if gate2d is None else RET_HEADS),
        grid=(t // tm,),
        in_specs=[
            pl.BlockSpec((tm, d), row),
            *[pl.BlockSpec((tm, a.shape[1]), row) for a in mixer],
            pl.BlockSpec((None, tm, p3d.shape[2]), lambda i: (layer, i, 0)),
            _resident(w_out.shape),
            _layer_of(mlp_gain.shape, layer),
            _layer_of(w1.shape, layer),
            _layer_of(w2.shape, layer),
            _layer_of(ple_gain.shape, layer),
            _layer_of(wg.shape, layer),
            _layer_of(wp.shape, layer),
        ],
        out_specs=pl.BlockSpec((tm, d), row),
        out_shape=jax.ShapeDtypeStruct((t, d), F32),
        compiler_params=_params("parallel"),
        name="layer_tail",
    )(h2d, *mixer, p3d, w_out.astype(MXU_DTYPE), mlp_gain, w1, w2, ple_gain, wg, wp)


def _mla_proj_kernel(x_ref, g_ref, win_ref, qa_ref, kva_ref, wuq_ref, wukv_ref, qn_ref, kn_ref,
                     qrope_ref, krope_ref, q_ref, k_ref, v_ref, *, sub_tiles):
    tm = x_ref.shape[0] // sub_tiles
    d = x_ref.shape[-1]
    lane = lax.broadcasted_iota(jnp.int32, (1, LANES), 1)
    first_half = (lane < MLA_ROPE).astype(F32)
    softmax_scale = MLA_QKD ** -0.5 * LOG2_E
    qn = qn_ref[...]
    kn = kn_ref[...]
    ones = jnp.ones((MLA_NOPE, LANES), MXU_DTYPE)
    halves = jnp.full((LANES, LANES), 0.5, MXU_DTYPE)
    head_weights = jnp.concatenate([ones, halves], axis=0)

    def down(rows):
        x = x_ref[rows, :]
        inv = lax.rsqrt(_row_sumsq(x) * (1.0 / d) + EPS)
        proj = _dot((x * g_ref[...]).astype(MXU_DTYPE), win_ref[...])
        proj = proj * _lanes(inv, proj.shape[-1])
        c_q = proj[:, :MLA_Q_RANK]
        c_kv = proj[:, MLA_Q_RANK:MLA_Q_RANK + MLA_KV_RANK]
        k_r = proj[:, MLA_Q_RANK + MLA_KV_RANK:]
        inv = lax.rsqrt(_row_sumsq(c_q) * (1.0 / MLA_Q_RANK) + EPS)
        c_q = (c_q * _lanes(inv, MLA_Q_RANK) * qa_ref[...]).astype(MXU_DTYPE)
        inv = lax.rsqrt(_row_sumsq(c_kv) * (1.0 / MLA_KV_RANK) + EPS)
        c_kv = (c_kv * _lanes(inv, MLA_KV_RANK) * kva_ref[...]).astype(MXU_DTYPE)
        k_rot = k_r * krope_ref[rows, :]
        k_rot = (k_rot + pltpu.roll(k_rot, MLA_ROPE, 1)) * first_half
        return c_q, c_kv, k_rot, _row_sumsq(k_r, halves)

    tiles = [slice(i * tm, (i + 1) * tm) for i in range(sub_tiles)]
    projected = [down(rows) for rows in tiles]

    def up_project(t, h):
        c_q, c_kv = projected[t][:2]
        cols = slice(h * MLA_HEAD_W, (h + 1) * MLA_HEAD_W)
        return _dot(c_q, wuq_ref[:, cols]), _dot(c_kv, wukv_ref[:, cols])

    def normalise(t, h, qh, kvh):
        rows = tiles[t]
        k_rot, k_r_ss = projected[t][2:]
        lo = h * MLA_HEAD_W
        q_nope, q_r = qh[:, :MLA_NOPE], qh[:, MLA_NOPE:]
        inv = lax.rsqrt(_row_sumsq(qh, head_weights) * (1.0 / MLA_QKD) + EPS) * softmax_scale
        q_rot = q_r * qrope_ref[rows, :]
        q_rot = q_rot + pltpu.roll(q_rot, MLA_ROPE, 1)
        q_ref[rows, lo:lo + MLA_NOPE] = (q_nope * inv * qn).astype(q_ref.dtype)
        q_ref[rows, lo + MLA_NOPE:lo + MLA_HEAD_W] = (q_rot * inv).astype(q_ref.dtype)

        k_nope, v = kvh[:, :MLA_NOPE], kvh[:, MLA_NOPE:]
        inv = lax.rsqrt((_row_sumsq(k_nope, ones) + k_r_ss) * (1.0 / MLA_QKD) + EPS)
        k_ref[rows, lo:lo + MLA_NOPE] = (k_nope * inv * kn).astype(k_ref.dtype)
        k_ref[rows, lo + MLA_NOPE:lo + MLA_HEAD_W] = (k_rot * inv).astype(k_ref.dtype)
        v_ref[rows, h * MLA_VD:(h + 1) * MLA_VD] = v.astype(v_ref.dtype)

    units = [(t, h) for t in range(sub_tiles) for h in range(MLA_HEADS)]
    ahead = up_project(*units[0])
    for u, unit in enumerate(units):
        current = ahead
        if u + 1 < len(units):
            ahead = up_project(*units[u + 1])
        normalise(*unit, *current)


def _rotate_half_cols(w):
    half = w.shape[-1] // 2
    return jnp.concatenate([-w[..., half:], w[..., :half]], axis=-1)


def _rope_gain_table(seq, gain_rope):
    cos, sin = _rope_angles(seq, MLA_ROPE)
    base = jnp.concatenate([cos, cos, sin, sin], axis=-1)
    half = MLA_ROPE // 2
    gains = jnp.concatenate([gain_rope, gain_rope[half:], gain_rope[:half]])
    return base * gains[None, :]


def _mla_proj(x2d, gain, w_in, q_a_gain, kv_a_gain, w_uq, w_ukv, q_gain, k_gain, seq, tm, sub_tiles):
    t, d = x2d.shape
    kr_lo = MLA_Q_RANK + MLA_KV_RANK
    w_in_ext = jnp.concatenate([w_in, _rotate_half_cols(w_in[:, kr_lo:])], axis=1)
    w_uq_h = w_uq.reshape(MLA_Q_RANK, MLA_HEADS, MLA_QKD)
    w_uq_ext = jnp.concatenate([w_uq_h, _rotate_half_cols(w_uq_h[..., MLA_NOPE:])], axis=-1)
    w_uq_ext = w_uq_ext.reshape(MLA_Q_RANK, MLA_HEADS * MLA_HEAD_W)
    q_rope = _rope_gain_table(seq, q_gain[MLA_NOPE:])
    k_rope = _rope_gain_table(seq, k_gain[MLA_NOPE:])
    qk_w = MLA_HEADS * MLA_HEAD_W
    v_w = MLA_HEADS * MLA_VD
    block = sub_tiles * tm
    pos_blocks = seq // block
    row = lambda i: (i, 0)
    pos = lambda i: (i % pos_blocks, 0)
    return pl.pallas_call(
        functools.partial(_mla_proj_kernel, sub_tiles=sub_tiles),
        grid=(t // block,),
        in_specs=[
            pl.BlockSpec((block, d), row),
            _resident((1, d)),
            _resident(w_in_ext.shape),
            _resident((1, MLA_Q_RANK)),
            _resident((1, MLA_KV_RANK)),
            _resident(w_uq_ext.shape),
            _resident(w_ukv.shape),
            _resident((1, MLA_NOPE)),
            _resident((1, MLA_NOPE)),
            pl.BlockSpec((block, LANES), pos),
            pl.BlockSpec((block, LANES), pos),
        ],
        out_specs=[
            pl.BlockSpec((block, qk_w), row),
            pl.BlockSpec((block, qk_w), row),
            pl.BlockSpec((block, v_w), row),
        ],
        out_shape=[
            jax.ShapeDtypeStruct((t, qk_w), MXU_DTYPE),
            jax.ShapeDtypeStruct((t, qk_w), MXU_DTYPE),
            jax.ShapeDtypeStruct((t, v_w), MXU_DTYPE),
        ],
        compiler_params=_params("parallel"),
        name="mla_proj",
    )(x2d, gain.reshape(1, d), w_in_ext.astype(MXU_DTYPE), q_a_gain.reshape(1, -1), kv_a_gain.reshape(1, -1),
      w_uq_ext.astype(MXU_DTYPE), w_ukv.astype(MXU_DTYPE), q_gain[:MLA_NOPE].reshape(1, -1),
      k_gain[:MLA_NOPE].reshape(1, -1), q_rope, k_rope)


def _attn_kernel(q_ref, k_ref, v_ref, o_ref, m_ref, acc_ref, *, block):
    n_blocks = q_ref.shape[1] // block
    half = block // 2

    units = []
    for qb in range(n_blocks):
        q0 = qb * block
        units += [((q0, block), (kb * block, block)) for kb in range(qb)]
        units += [((q0, half), (q0, half)), ((q0 + half, half), (q0, block))]

    def span(start_size):
        return slice(start_size[0], start_size[0] + start_size[1])

    def scores(q_rows, k_rows):
        return lax.dot_general(q_ref[0, span(q_rows), :], k_ref[0, span(k_rows), :], _NT,
                               preferred_element_type=F32)

    def update(s, q_rows, k_rows):
        (q0, nq), (k0, nk) = q_rows, k_rows
        slot = (q0 // block) % 2
        local = slice(q0 % block, q0 % block + nq)
        first = k0 == 0
        last = k0 + nk == q0 + nq
        if last:
            q_chunk = (q0 + lax.broadcasted_iota(jnp.int32, s.shape, 0)) // CHUNK
            k_chunk = (k0 + lax.broadcasted_iota(jnp.int32, s.shape, 1)) // CHUNK
            s = jnp.where(k_chunk <= q_chunk, s, -1e30)
        m_new = jnp.broadcast_to(jnp.max(s, axis=-1, keepdims=True), (nq, LANES))
        if not first:
            m_prev = m_ref[slot, local, :]
            m_new = jnp.maximum(m_prev, m_new)
        p = jnp.exp2(s - jnp.tile(m_new, (1, nk // LANES)))
        ones = jnp.ones((nk, LANES), MXU_DTYPE)
        acc = _dot(p.astype(MXU_DTYPE), jnp.concatenate([v_ref[0, span(k_rows), :], ones], axis=1))
        if not first:
            acc = jnp.tile(jnp.exp2(m_prev - m_new), (1, 2)) * acc_ref[slot, local, :] + acc
        if last:
            o_ref[0, span(q_rows), :] = (acc[:, :MLA_VD] / acc[:, MLA_VD:]).astype(o_ref.dtype)
        else:
            m_ref[slot, local, :] = m_new
            acc_ref[slot, local, :] = acc

    s_next = scores(*units[0])
    for u, unit in enumerate(units):
        s = s_next
        if u + 1 < len(units):
            s_next = scores(*units[u + 1])
        update(s, *unit)


def _attention(q, k, v, block):
    b, s, _ = q.shape
    return pl.pallas_call(
        functools.partial(_attn_kernel, block=block),
        grid=(b, MLA_HEADS),
        in_specs=[
            pl.BlockSpec((1, s, MLA_HEAD_W), lambda i, h: (i, 0, h)),
            pl.BlockSpec((1, s, MLA_HEAD_W), lambda i, h: (i, 0, h)),
            pl.BlockSpec((1, s, MLA_VD), lambda i, h: (i, 0, h)),
        ],
        out_specs=pl.BlockSpec((1, s, MLA_VD), lambda i, h: (i, 0, h)),
        out_shape=jax.ShapeDtypeStruct((b, s, MLA_HEADS * MLA_VD), MXU_DTYPE),
        scratch_shapes=[
            pltpu.VMEM((2, block, LANES), F32),
            pltpu.VMEM((2, block, MLA_VD + LANES), F32),
        ],
        compiler_params=_params("parallel", "parallel"),
        name="mla_attention",
    )(q, k, v)


def _tiles(seq):
    assert seq % RET_SUPER == 0, seq
    tm = 512 if seq % 512 == 0 else RET_SUPER
    mla_sub_tiles = 2 if seq % (2 * tm) == 0 else 1
    ret_block = 2 * tm if seq % (2 * tm) == 0 else tm
    return dict(tm=tm, ret_block=ret_block, attn_block=tm, ff_chunk=512, mla_sub_tiles=mla_sub_tiles)


def kernel(x, p, mix_norm, ret_w_in, ret_gn, ret_w_out, mla_w_in, mla_q_a_norm, mla_kv_a_norm, mla_w_uq,
           mla_w_ukv, mla_q_norm, mla_k_norm, mla_w_out, mlp_norm, mlp_w1, mlp_w2, ple_norm, ple_gate_w,
           ple_proj_w):
    b, s, d = x.shape
    depth = p.shape[0]
    t = b * s
    cfg = _tiles(s)
    tm = cfg["tm"]
    h = x.reshape(t, d)
    p3d = p.reshape(depth, t, -1)
    mlp_gain = mlp_norm.reshape(depth, 1, d)
    ple_gain = ple_norm.reshape(depth, 1, d)
    w1, w2, wg, wp = (w.astype(MXU_DTYPE) for w in (mlp_w1, mlp_w2, ple_gate_w, ple_proj_w))
    for i in range(depth):
        j = i // 2
        if i % 2 == 0:
            q, k, v, gate = _ret_proj(h, mix_norm[i], ret_w_in[j], ret_gn[j], s, cfg["ret_block"])
            y = _retention(q.reshape(b, s, -1), k.reshape(b, s, -1), v.reshape(b, s, -1),
                           4 * tm if s % (4 * tm) == 0 else cfg["ret_block"])
            w_out = ret_w_out[j]
        else:
            q, k, v = _mla_proj(h, mix_norm[i], mla_w_in[j], mla_q_a_norm[j], mla_kv_a_norm[j], mla_w_uq[j],
                                mla_w_ukv[j], mla_q_norm[j], mla_k_norm[j], s, tm, cfg["mla_sub_tiles"])
            gate = None
            y = _attention(q.reshape(b, s, -1), k.reshape(b, s, -1), v.reshape(b, s, -1), cfg["attn_block"])
            w_out = mla_w_out[j]
        h = _tail(h, y.reshape(t, -1), gate, p3d, w_out, mlp_gain, w1, w2, ple_gain, wg, wp, i, tm,
                  cfg["ff_chunk"])
    return h.reshape(b, s, d)
```
